```python
import math
import jax, jax.numpy as jnp
from jax import lax
import numpy as np

D_MODEL = 1024
BATCH = 2
SEQ = 8192
DEPTH = 2
DEC_BATCH = 128
DEC_SEQ = 1
PAST_LEN = 16384
PAGE_SIZE = 128

HEAD_DIM = 64
MLA_HEADS = 8
MLA_Q_LORA = 384
MLA_KV_LORA = 256
MLA_NOPE = 64
MLA_ROPE = 32
MLA_V = 64
ROPE_THETA = 10000.0
FOX_HEADS = 4
MOBA_HEADS = 4
MOBA_BLOCK = 256
MOBA_TOPK = 3
Q_BLOCK = 128
MIX_WIDTH = MLA_HEADS * MLA_V + FOX_HEADS * HEAD_DIM + MOBA_HEADS * HEAD_DIM
D_FF = ((8 * D_MODEL + 3 * 256 - 1) // (3 * 256)) * 256
DEEPNORM_ALPHA = (2.0 * DEPTH) ** 0.25
DEEPNORM_BETA = (8.0 * DEPTH) ** -0.25
FORGET_BIAS_INIT = 4.0
LN_EPS = 1e-5
RMS_EPS = 1e-6
IN_SPLITS = (MLA_Q_LORA, MLA_KV_LORA, MLA_ROPE,
             FOX_HEADS * HEAD_DIM, HEAD_DIM, HEAD_DIM, FOX_HEADS,
             MOBA_HEADS * HEAD_DIM, HEAD_DIM, HEAD_DIM)
IN_COLS = sum(IN_SPLITS)
IN_OFFSETS = tuple(int(o) for o in np.cumsum(IN_SPLITS)[:-1])

kernel_name = "hymba_mla_fox_moba_deepnorm_step"


def layer_norm(x, g, b):
    xf = x.astype(jnp.float32)
    mu = xf.mean(-1, keepdims=True)
    var = jnp.square(xf - mu).mean(-1, keepdims=True)
    return ((xf - mu) * lax.rsqrt(var + LN_EPS) * g + b).astype(x.dtype)


def rms_norm(x, g):
    xf = x.astype(jnp.float32)
    return (xf * lax.rsqrt(jnp.mean(xf * xf, -1, keepdims=True) + RMS_EPS) * g).astype(x.dtype)


def apply_rope(x, pos):
    half = MLA_ROPE // 2
    inv = ROPE_THETA ** (-jnp.arange(half, dtype=jnp.float32) / half)
    ang = pos.astype(jnp.float32)[:, None] * inv[None, :]
    cos = jnp.cos(ang)[None, :, None, :]
    sin = jnp.sin(ang)[None, :, None, :]
    xf = x.astype(jnp.float32)
    x1, x2 = xf[..., :half], xf[..., half:]
    return jnp.concatenate([x1 * cos - x2 * sin, x2 * cos + x1 * sin], -1).astype(x.dtype)


def alibi_slopes():
    return 2.0 ** (-8.0 * jnp.arange(1, MOBA_HEADS + 1, dtype=jnp.float32) / MOBA_HEADS)


def project_mixers(x, pos, w_in, b_forget, q_norm, w_q_up, kv_norm, w_uk):
    B, T, _ = x.shape
    h = jnp.einsum('btd,dc->btc', x, w_in)
    cq, ckv, kpe, fq, fk, fv, fg, mq, mk, mv = jnp.split(h, IN_OFFSETS, axis=-1)
    q = jnp.einsum('btc,che->bthe', rms_norm(cq, q_norm), w_q_up)
    q_lat = jnp.einsum('bthn,chn->bthc', q[..., :MLA_NOPE], w_uk)
    q_pe = apply_rope(q[..., MLA_NOPE:], pos)
    ckv = rms_norm(ckv, kv_norm)
    kpe = apply_rope(kpe[:, :, None, :], pos)[:, :, 0, :]
    fq = fq.reshape(B, T, FOX_HEADS, HEAD_DIM)
    logf = jax.nn.log_sigmoid(fg.astype(jnp.float32) + b_forget)
    mq = mq.reshape(B, T, MOBA_HEADS, HEAD_DIM)
    return q_lat, q_pe, ckv, kpe, fq, fk, fv, logf, mq, mk, mv


def mla_attend(q_lat, q_pe, ckv, kpe, q_pos, k_pos):
    scale = (MLA_NOPE + MLA_ROPE) ** -0.5
    s = (jnp.einsum('bqhc,bkc->bhqk', q_lat, ckv)
         + jnp.einsum('bqhr,bkr->bhqk', q_pe, kpe)).astype(jnp.float32) * scale
    s = jnp.where(k_pos[None, None, None, :] <= q_pos[None, None, :, None], s, -jnp.inf)
    p = jax.nn.softmax(s, axis=-1).astype(ckv.dtype)
    return jnp.einsum('bhqk,bkc->bqhc', p, ckv)


def fox_attend(q, k, v, cum_q, cum_k, q_pos, k_pos):
    s = jnp.einsum('bqhd,bkd->bhqk', q, k).astype(jnp.float32) * HEAD_DIM ** -0.5
    s = s + jnp.swapaxes(cum_q, 1, 2)[..., None] - jnp.swapaxes(cum_k, 1, 2)[:, :, None, :]
    s = jnp.where(k_pos[None, None, None, :] <= q_pos[None, None, :, None], s, -jnp.inf)
    p = jax.nn.softmax(s, axis=-1).astype(v.dtype)
    return jnp.einsum('bhqk,bkd->bqhd', p, v)


def to_blocks(k):
    B, L, d = k.shape
    nb = -(-L // MOBA_BLOCK)
    k = jnp.pad(k, ((0, 0), (0, nb * MOBA_BLOCK - L), (0, 0)))
    return k.reshape(B, nb, MOBA_BLOCK, d)


def block_means(kb):
    return kb.astype(jnp.float32).mean(axis=2).astype(kb.dtype)


def moba_attend(q, kb, vb, k_mean, q_pos, slopes):
    B, NB = kb.shape[0], kb.shape[1]
    H = q.shape[2]
    Q = q.shape[1]
    scale = HEAD_DIM ** -0.5
    own = q_pos // MOBA_BLOCK
    blk = jnp.arange(NB, dtype=jnp.int32)
    gate = jnp.einsum('bqhd,bnd->bhqn', q, k_mean).astype(jnp.float32)
    gate = jnp.where(blk[None, None, None, :] < own[None, None, :, None], gate, -jnp.inf)
    topk = min(MOBA_TOPK, NB)
    _, sel = lax.top_k(gate, topk)
    valid = sel < own[None, None, :, None]
    bidx = jnp.arange(B)[:, None, None, None]
    k_sel = kb[bidx, sel]
    v_sel = vb[bidx, sel]
    offs = jnp.arange(MOBA_BLOCK, dtype=jnp.int32)
    dist_sel = (q_pos[None, None, :, None, None] - (sel[..., None] * MOBA_BLOCK + offs)).astype(jnp.float32)
    s_sel = (jnp.einsum('bqhd,bhqnpd->bhqnp', q, k_sel).astype(jnp.float32) * scale
             - slopes[None, :, None, None, None] * dist_sel)
    s_sel = jnp.where(valid[..., None], s_sel, -jnp.inf).reshape(B, H, Q, topk * MOBA_BLOCK)
    k_own = kb[:, own]
    v_own = vb[:, own]
    dist_own = (q_pos[:, None] - (own[:, None] * MOBA_BLOCK + offs[None, :])).astype(jnp.float32)
    s_own = (jnp.einsum('bqhd,bqpd->bhqp', q, k_own).astype(jnp.float32) * scale
             - slopes[None, :, None, None] * dist_own[None, None])
    s_own = jnp.where((dist_own >= 0)[None, None], s_own, -jnp.inf)
    p = jax.nn.softmax(jnp.concatenate([s_sel, s_own], -1), axis=-1).astype(vb.dtype)
    p_sel = p[..., :topk * MOBA_BLOCK].reshape(B, H, Q, topk, MOBA_BLOCK)
    p_own = p[..., topk * MOBA_BLOCK:]
    return (jnp.einsum('bhqnp,bhqnpd->bqhd', p_sel, v_sel)
            + jnp.einsum('bhqp,bqpd->bqhd', p_own, v_own))


def mix_heads(qs, keys, q_pos, k_pos, w_uv, slopes):
    q_lat, q_pe, fq, fcum_q, mq = qs
    ckv, kpe, fk, fv, fcum_k, mkb, mvb, mk_mean = keys
    B, Q = q_lat.shape[0], q_lat.shape[1]
    o_mla = jnp.einsum('bqhc,chv->bqhv', mla_attend(q_lat, q_pe, ckv, kpe, q_pos, k_pos), w_uv)
    o_fox = fox_attend(fq, fk, fv, fcum_q, fcum_k, q_pos, k_pos)
    o_moba = moba_attend(mq, mkb, mvb, mk_mean, q_pos, slopes)
    return jnp.concatenate([o_mla.reshape(B, Q, MLA_HEADS * MLA_V),
                            o_fox.reshape(B, Q, FOX_HEADS * HEAD_DIM),
                            o_moba.reshape(B, Q, MOBA_HEADS * HEAD_DIM)], axis=-1)


def residual_out(x, mix, w_out, ln_mix_g, ln_mix_b, w_gate, w_up, w_down, ln_ffn_g, ln_ffn_b):
    h = jnp.einsum('btm,md->btd', mix.astype(x.dtype), w_out)
    x = layer_norm(DEEPNORM_ALPHA * x + h, ln_mix_g, ln_mix_b)
    g = jnp.einsum('btd,df->btf', x, w_gate)
    u = jnp.einsum('btd,df->btf', x, w_up)
    f = jnp.einsum('btf,fd->btd', jax.nn.silu(g) * u, w_down)
    return layer_norm(DEEPNORM_ALPHA * x + f, ln_ffn_g, ln_ffn_b)


def setup_inputs(seed: int = 0) -> dict:
    key = jax.random.key(seed)
    ks = iter(jax.random.split(key, 32))

    def nrm(shape, scale):
        return jax.random.normal(next(ks), shape, jnp.float32) * scale

    n_pages = PAST_LEN // PAGE_SIZE
    n_used = DEC_BATCH * n_pages
    n_phys = n_used + n_used // 4
    pool = (DEPTH, n_phys, PAGE_SIZE)
    return {
        "x_prompt": nrm((BATCH, SEQ, D_MODEL), 1.0),
        "x_sample": nrm((DEC_BATCH, DEC_SEQ, D_MODEL), 1.0),
        "cache_mla_ckv": nrm(pool + (MLA_KV_LORA,), 1.0),
        "cache_mla_kpe": nrm(pool + (MLA_ROPE,), 1.0),
        "cache_fox_k": nrm(pool + (HEAD_DIM,), 1.0),
        "cache_fox_v": nrm(pool + (HEAD_DIM,), 1.0),
        "cache_fox_logf": jax.nn.log_sigmoid(FORGET_BIAS_INIT + nrm(pool + (FOX_HEADS,), 1.0)),
        "cache_moba_k": nrm(pool + (HEAD_DIM,), 1.0),
        "cache_moba_v": nrm(pool + (HEAD_DIM,), 1.0),
        "page_table": jax.random.permutation(next(ks), n_phys)[:n_used].reshape(DEC_BATCH, n_pages).astype(jnp.int32),
        "w_in": nrm((DEPTH, D_MODEL, IN_COLS), D_MODEL ** -0.5),
        "b_forget": FORGET_BIAS_INIT + nrm((DEPTH, FOX_HEADS), 0.5),
        "mla_q_norm": 1.0 + nrm((DEPTH, MLA_Q_LORA), 0.02),
        "w_q_up": nrm((DEPTH, MLA_Q_LORA, MLA_HEADS, MLA_NOPE + MLA_ROPE), MLA_Q_LORA ** -0.5),
        "mla_kv_norm": 1.0 + nrm((DEPTH, MLA_KV_LORA), 0.02),
        "w_uk": nrm((DEPTH, MLA_KV_LORA, MLA_HEADS, MLA_NOPE), MLA_KV_LORA ** -0.5),
        "w_uv": nrm((DEPTH, MLA_KV_LORA, MLA_HEADS, MLA_V), MLA_KV_LORA ** -0.5),
        "w_out": nrm((DEPTH, MIX_WIDTH, D_MODEL), MIX_WIDTH ** -0.5 * DEEPNORM_BETA),
        "ln_mix_g": 1.0 + nrm((DEPTH, D_MODEL), 0.02),
        "ln_mix_b": nrm((DEPTH, D_MODEL), 0.02),
        "w_gate": nrm((DEPTH, D_MODEL, D_FF), D_MODEL ** -0.5),
        "w_up": nrm((DEPTH, D_MODEL, D_FF), D_MODEL ** -0.5),
        "w_down": nrm((DEPTH, D_FF, D_MODEL), D_FF ** -0.5 * DEEPNORM_BETA),
        "ln_ffn_g": 1.0 + nrm((DEPTH, D_MODEL), 0.02),
        "ln_ffn_b": nrm((DEPTH, D_MODEL), 0.02),
    }


def reference(x_prompt, x_sample, cache_mla_ckv, cache_mla_kpe, cache_fox_k, cache_fox_v, cache_fox_logf,
              cache_moba_k, cache_moba_v, page_table, w_in, b_forget, mla_q_norm, w_q_up, mla_kv_norm,
              w_uk, w_uv, w_out, ln_mix_g, ln_mix_b, w_gate, w_up, w_down, ln_ffn_g, ln_ffn_b):
    B, S, _ = x_prompt.shape
    Bd, T, _ = x_sample.shape
    past = page_table.shape[1] * PAGE_SIZE
    slopes = alibi_slopes()
    pos_p = jnp.arange(S, dtype=jnp.int32)
    pos_s = past + jnp.arange(T, dtype=jnp.int32)
    k_pos_s = jnp.arange(past + T, dtype=jnp.int32)
    n_qblocks = S // Q_BLOCK
    xp, xs = x_prompt, x_sample
    new_p = [[] for _ in range(7)]
    new_s = [[] for _ in range(7)]

    for l in range(DEPTH):
        lw = (w_in[l], b_forget[l], mla_q_norm[l], w_q_up[l], mla_kv_norm[l], w_uk[l])
        tail = (w_out[l], ln_mix_g[l], ln_mix_b[l], w_gate[l], w_up[l], w_down[l], ln_ffn_g[l], ln_ffn_b[l])

        q_lat, q_pe, ckv, kpe, fq, fk, fv, logf, mq, mk, mv = project_mixers(xp, pos_p, *lw)
        cum = jnp.cumsum(logf, axis=1)
        mkb, mvb = to_blocks(mk), to_blocks(mv)
        keys_p = (ckv, kpe, fk, fv, cum, mkb, mvb, block_means(mkb))

        def one_block(i):
            start = i * Q_BLOCK
            qs = tuple(lax.dynamic_slice_in_dim(a, start, Q_BLOCK, axis=1) for a in (q_lat, q_pe, fq, cum, mq))
            q_pos = start + jnp.arange(Q_BLOCK, dtype=jnp.int32)
            return mix_heads(qs, keys_p, q_pos, pos_p, w_uv[l], slopes)

        mix_p = lax.map(one_block, jnp.arange(n_qblocks, dtype=jnp.int32))
        mix_p = jnp.moveaxis(mix_p, 0, 1).reshape(B, S, MIX_WIDTH)
        for lst, a in zip(new_p, (ckv, kpe, fk, fv, logf, mk, mv)):
            lst.append(a)
        xp = residual_out(xp, mix_p, *tail)

        s_lat, s_pe, sckv, skpe, sfq, sfk, sfv, slogf, smq, smk, smv = project_mixers(xs, pos_s, *lw)

        def gather(cache):
            return cache[l, page_table].reshape(Bd, past, cache.shape[-1])

        ckv_all = jnp.concatenate([gather(cache_mla_ckv), sckv.astype(cache_mla_ckv.dtype)], axis=1)
        kpe_all = jnp.concatenate([gather(cache_mla_kpe), skpe.astype(cache_mla_kpe.dtype)], axis=1)
        fk_all = jnp.concatenate([gather(cache_fox_k), sfk.astype(cache_fox_k.dtype)], axis=1)
        fv_all = jnp.concatenate([gather(cache_fox_v), sfv.astype(cache_fox_v.dtype)], axis=1)
        logf_all = jnp.concatenate([gather(cache_fox_logf).astype(jnp.float32), slogf], axis=1)
        cum_all = jnp.cumsum(logf_all, axis=1)
        mkb_s = to_blocks(jnp.concatenate([gather(cache_moba_k), smk.astype(cache_moba_k.dtype)], axis=1))
        mvb_s = to_blocks(jnp.concatenate([gather(cache_moba_v), smv.astype(cache_moba_v.dtype)], axis=1))
        keys_s = (ckv_all, kpe_all, fk_all, fv_all, cum_all, mkb_s, mvb_s, block_means(mkb_s))
        qs_s = (s_lat, s_pe, sfq, cum_all[:, past:], smq)
        mix_s = mix_heads(qs_s, keys_s, pos_s, k_pos_s, w_uv[l], slopes)
        for lst, a in zip(new_s, (sckv, skpe, sfk, sfv, slogf, smk, smv)):
            lst.append(a)
        xs = residual_out(xs, mix_s, *tail)

    new_mla_ckv_p, new_mla_kpe_p, new_fox_k_p, new_fox_v_p, new_fox_logf_p, new_moba_k_p, new_moba_v_p = [jnp.stack(a, 0) for a in new_p]
    new_mla_ckv_s, new_mla_kpe_s, new_fox_k_s, new_fox_v_s, new_fox_logf_s, new_moba_k_s, new_moba_v_s = [jnp.stack(a, 0) for a in new_s]
    return (xp, xs,
            new_mla_ckv_p, new_mla_ckv_s,
            new_mla_kpe_p, new_mla_kpe_s,
            new_fox_k_p, new_fox_k_s,
            new_fox_v_p, new_fox_v_s,
            new_fox_logf_p, new_fox_logf_s,
            new_moba_k_p, new_moba_k_s,
            new_moba_v_p, new_moba_v_s)
```

```python
import functools

import numpy as np
import jax
import jax.numpy as jnp
from jax import lax
from jax.experimental import pallas as pl
from jax.experimental.pallas import tpu as pltpu

F32 = jnp.float32
BF16 = jnp.bfloat16

D_MODEL = 1024
HEAD_DIM = 64
MLA_HEADS = 8
MLA_Q_LORA = 384
MLA_KV_LORA = 256
MLA_NOPE = 64
MLA_ROPE = 32
MLA_V = 64
ROPE_THETA = 10000.0
FOX_HEADS = 4
MOBA_HEADS = 4
MOBA_BLOCK = 256
MOBA_TOPK = 3
PAGE = 128
DEPTH = 2
D_FF = 2816
ALPHA = (2.0 * DEPTH) ** 0.25
LN_EPS = 1e-5
RMS_EPS = 1e-6
MLA_SCALE = (MLA_NOPE + MLA_ROPE) ** -0.5
ATT_SCALE = HEAD_DIM ** -0.5

LANES = 128
NEG = -1e30
VMEM_LIMIT = 56 * 1024 * 1024

C_CQ = 0
C_CKV = 384
C_KPE = 640
C_KPE_SW = 768
C_FQ = 896
C_MQ = 1408
C_FKV = 1920
C_MKV = 2048
C_FG = 2176
N_COLS = 2304

FOX_NEGC = 64
FOX_POSC = 76
MOBA_BLK = 64
MOBA_ALI = 96


def _cparams(n_axes):
    return pltpu.CompilerParams(dimension_semantics=("arbitrary",) * n_axes, vmem_limit_bytes=VMEM_LIMIT)


def _split3(x):
    a = x.astype(BF16)
    r = x - a.astype(F32)
    b = r.astype(BF16)
    c = (r - b.astype(F32)).astype(BF16)
    return a, b, c


def _nt_dot(a, b):
    return lax.dot_general(a, b, (((1,), (1,)), ((), ())), preferred_element_type=F32)


def _proj_body(x_ref, cosk_ref, sink_ref, cosq_ref, sinq_ref, wall_ref, gq_ref, gkv_ref, wq2_ref, wuk_ref, bfg_ref,
               qmla_ref, kmla_ref, ckv_ref, kpe_ref, qf_ref, qm_ref, fkv_ref, mkv_ref, logf_ref):
    tm = x_ref.shape[0]
    h = jnp.dot(x_ref[...].astype(BF16), wall_ref[...], preferred_element_type=F32)
    lane = lax.broadcasted_iota(jnp.int32, (tm, LANES), 1)

    cq = h[:, C_CQ:C_CQ + MLA_Q_LORA]
    cqn = cq * lax.rsqrt(jnp.mean(cq * cq, axis=-1, keepdims=True) + RMS_EPS) * gq_ref[...]
    q = jnp.dot(cqn.astype(BF16), wq2_ref[...], preferred_element_type=F32)
    qpe = (q[:, 512:768] * cosq_ref[...] + q[:, 768:1024] * sinq_ref[...]) * MLA_SCALE
    qn = q[:, 0:512].astype(BF16)
    for hh in range(MLA_HEADS):
        pair = hh // 2
        qlat = jnp.dot(qn[:, LANES * pair:LANES * (pair + 1)], wuk_ref[hh], preferred_element_type=F32)
        qmla_ref[0, hh, :, 0:MLA_KV_LORA] = (qlat * MLA_SCALE).astype(BF16)
        grp, sub = hh // 4, hh % 4
        chunk = qpe[:, LANES * grp:LANES * (grp + 1)]
        chunk = jnp.where((lane >> 5) == sub, chunk, 0.0)
        qmla_ref[0, hh, :, MLA_KV_LORA:MLA_KV_LORA + LANES] = chunk.astype(BF16)

    ckv = h[:, C_CKV:C_CKV + MLA_KV_LORA]
    ckvn = ckv * lax.rsqrt(jnp.mean(ckv * ckv, axis=-1, keepdims=True) + RMS_EPS) * gkv_ref[...]
    ckv_ref[...] = ckvn
    kpe = h[:, C_KPE:C_KPE + LANES] * cosk_ref[...] + h[:, C_KPE_SW:C_KPE_SW + LANES] * sink_ref[...]
    kpe_ref[...] = kpe
    kmla_ref[:, 0:MLA_KV_LORA] = ckvn.astype(BF16)
    kmla_ref[:, MLA_KV_LORA:MLA_KV_LORA + LANES] = kpe.astype(BF16)

    for hh in range(FOX_HEADS):
        qf_ref[0, hh] = (h[:, C_FQ + LANES * hh:C_FQ + LANES * (hh + 1)] * ATT_SCALE).astype(BF16)
    for hh in range(MOBA_HEADS):
        qm_ref[0, hh] = (h[:, C_MQ + LANES * hh:C_MQ + LANES * (hh + 1)] * ATT_SCALE).astype(BF16)
    fkv_ref[...] = h[:, C_FKV:C_FKV + LANES]
    mkv_ref[...] = h[:, C_MKV:C_MKV + LANES]

    z = h[:, C_FG:C_FG + LANES] + bfg_ref[...]
    logsig = jnp.minimum(z, 0.0) - jnp.log(1.0 + jnp.exp(-jnp.abs(z)))
    logf_ref[...] = jnp.where(lane < FOX_HEADS, logsig, 0.0)


def _proj(x2d, tabs, lw, nb, sp, tm):
    n = x2d.shape[0]
    nt = sp // tm
    cosk, sink, cosq, sinq = tabs
    row = lambda i: (i, 0)
    tab = lambda i: (i % nt, 0)
    c2 = lambda i: (0, 0)
    c3 = lambda i: (0, 0, 0)
    hd = lambda i: (i // nt, 0, i % nt, 0)
    in_specs = [
        pl.BlockSpec((tm, D_MODEL), row),
        pl.BlockSpec((tm, LANES), tab), pl.BlockSpec((tm, LANES), tab),
        pl.BlockSpec((tm, 256), tab), pl.BlockSpec((tm, 256), tab),
        pl.BlockSpec((D_MODEL, N_COLS), c2),
        pl.BlockSpec((1, MLA_Q_LORA), c2), pl.BlockSpec((1, MLA_KV_LORA), c2),
        pl.BlockSpec((MLA_Q_LORA, 1024), c2),
        pl.BlockSpec((MLA_HEADS, LANES, MLA_KV_LORA), c3),
        pl.BlockSpec((1, LANES), c2),
    ]
    dk = MLA_KV_LORA + LANES
    out_shape = [
        jax.ShapeDtypeStruct((nb, MLA_HEADS, sp, dk), BF16),
        jax.ShapeDtypeStruct((n, dk), BF16),
        jax.ShapeDtypeStruct((n, MLA_KV_LORA), F32),
        jax.ShapeDtypeStruct((n, LANES), F32),
        jax.ShapeDtypeStruct((nb, FOX_HEADS, sp, LANES), BF16),
        jax.ShapeDtypeStruct((nb, MOBA_HEADS, sp, LANES), BF16),
        jax.ShapeDtypeStruct((n, LANES), F32),
        jax.ShapeDtypeStruct((n, LANES), F32),
        jax.ShapeDtypeStruct((n, LANES), F32),
    ]
    out_specs = [
        pl.BlockSpec((1, MLA_HEADS, tm, dk), hd),
        pl.BlockSpec((tm, dk), row),
        pl.BlockSpec((tm, MLA_KV_LORA), row),
        pl.BlockSpec((tm, LANES), row),
        pl.BlockSpec((1, FOX_HEADS, tm, LANES), hd),
        pl.BlockSpec((1, MOBA_HEADS, tm, LANES), hd),
        pl.BlockSpec((tm, LANES), row),
        pl.BlockSpec((tm, LANES), row),
        pl.BlockSpec((tm, LANES), row),
    ]
    return pl.pallas_call(
        _proj_body, grid=(n // tm,), in_specs=in_specs, out_specs=out_specs, out_shape=out_shape,
        compiler_params=_cparams(1), name="proj",
    )(x2d, cosk, sink, cosq, sinq, lw["wall"], lw["gq"], lw["gkv"], lw["wq2"], lw["wuk"], lw["bfg"])


def _fox_prep_body(logf_ref, qf_ref, fkv_ref, tri_ref, pq_ref, pk_ref, qc_ref, kc_ref, qfox_ref, kfox_ref, carry_sc):
    t = pl.program_id(1)

    @pl.when(t == 0)
    def _():
        carry_sc[...] = jnp.zeros_like(carry_sc)

    tm = logf_ref.shape[0]
    tri = tri_ref[...]
    l1, l2, l3 = _split3(logf_ref[...])
    cum = (jnp.dot(tri, l1, preferred_element_type=F32) + jnp.dot(tri, l2, preferred_element_type=F32)
           + jnp.dot(tri, l3, preferred_element_type=F32)) + carry_sc[...]
    carry_sc[...] = cum[tm - 1:tm, :]
    c1, c2, c3 = _split3(cum)
    parts = jnp.concatenate([c1, c2, c3], axis=1)
    lane = lax.broadcasted_iota(jnp.int32, (tm, LANES), 1)
    kaug = jnp.dot(parts, pk_ref[...], preferred_element_type=F32) + kc_ref[...]
    kfox_ref[...] = (jnp.where(lane < HEAD_DIM, fkv_ref[...], 0.0) + kaug).astype(BF16)
    for hh in range(FOX_HEADS):
        qaug = jnp.dot(parts, pq_ref[hh], preferred_element_type=F32) + qc_ref[hh]
        qfox_ref[0, hh] = (qf_ref[0, hh].astype(F32) + qaug).astype(BF16)


def _fox_consts(tm):
    r = np.arange(tm)
    tri = (r[None, :] <= r[:, None]).astype(np.float32)
    pk = np.zeros((3 * LANES, LANES), np.float32)
    pq = np.zeros((FOX_HEADS, 3 * LANES, LANES), np.float32)
    kc = np.zeros((1, LANES), np.float32)
    qc = np.zeros((FOX_HEADS, 1, LANES), np.float32)
    for part in range(3):
        kc[0, FOX_POSC + part] = 1.0
        for hh in range(FOX_HEADS):
            pk[part * LANES + hh, FOX_NEGC + 4 * part + hh] = -1.0
            pq[hh, part * LANES + hh, FOX_POSC + part] = 1.0
            qc[hh, 0, FOX_NEGC + 4 * part + hh] = 1.0
    return (jnp.asarray(tri, BF16), jnp.asarray(pq, BF16), jnp.asarray(pk, BF16), jnp.asarray(qc), jnp.asarray(kc))


def _fox_prep(logf, qf, fkv, nb, sp, tm):
    nt = sp // tm
    tri, pq, pk, qc, kc = _fox_consts(tm)
    row = lambda b, t: (b * nt + t, 0)
    hd = lambda b, t: (b, 0, t, 0)
    c2 = lambda b, t: (0, 0)
    c3 = lambda b, t: (0, 0, 0)
    return pl.pallas_call(
        _fox_prep_body, grid=(nb, nt),
        in_specs=[pl.BlockSpec((tm, LANES), row), pl.BlockSpec((1, FOX_HEADS, tm, LANES), hd),
                  pl.BlockSpec((tm, LANES), row), pl.BlockSpec((tm, tm), c2),
                  pl.BlockSpec((FOX_HEADS, 3 * LANES, LANES), c3), pl.BlockSpec((3 * LANES, LANES), c2),
                  pl.BlockSpec((FOX_HEADS, 1, LANES), c3), pl.BlockSpec((1, LANES), c2)],
        out_specs=[pl.BlockSpec((1, FOX_HEADS, tm, LANES), hd), pl.BlockSpec((tm, LANES), row)],
        out_shape=[jax.ShapeDtypeStruct((nb, FOX_HEADS, sp, LANES), BF16),
                   jax.ShapeDtypeStruct((nb * sp, LANES), BF16)],
        scratch_shapes=[pltpu.VMEM((1, LANES), F32)],
        compiler_params=_cparams(2), name="fox_prep",
    )(logf, qf, fkv, tri, pq, pk, qc, kc)


def _moba_means_body(mkv_ref, o_ref):
    o_ref[0, 0] = jnp.sum(mkv_ref[...], axis=0, keepdims=True) * (1.0 / MOBA_BLOCK)


def _moba_means(mkv, nb, nblk):
    return pl.pallas_call(
        _moba_means_body, grid=(nb, nblk),
        in_specs=[pl.BlockSpec((MOBA_BLOCK, LANES), lambda b, n: (b * nblk + n, 0))],
        out_specs=pl.BlockSpec((1, 1, 1, LANES), lambda b, n: (b, n, 0, 0)),
        out_shape=jax.ShapeDtypeStruct((nb, nblk, 1, LANES), F32),
        compiler_params=_cparams(2), name="moba_means",
    )(mkv)


def _top_mask(gate, valid, lane, picks):
    g = jnp.where(valid, gate, -jnp.inf)
    sel = jnp.zeros(gate.shape, F32)
    for _ in range(picks):
        mx = jnp.max(g, axis=-1, keepdims=True)
        idx = jnp.min(jnp.where(g == mx, lane, 1 << 20), axis=-1, keepdims=True)
        pick = lane == idx
        sel = jnp.where(pick, jnp.where(valid, 1.0, sel), sel)
        g = jnp.where(pick, -jnp.inf, g)
    return sel


def _moba_prep_body(qm_ref, kmean_ref, mkv_ref, qaug_ref, kaug_ref, qmoba_ref, kmoba_ref):
    i = pl.program_id(1)
    tm = mkv_ref.shape[0]
    lane = lax.broadcasted_iota(jnp.int32, (tm, LANES), 1)
    kmean = kmean_ref[0].astype(BF16)
    past = (lane >= MOBA_BLK) & (lane < MOBA_BLK + i)
    in_blk = (lane >= MOBA_BLK) & (lane < MOBA_ALI)
    kmoba_ref[...] = (jnp.where(lane < HEAD_DIM, mkv_ref[...], 0.0) + kaug_ref[...]).astype(BF16)
    for hh in range(MOBA_HEADS):
        qh = qm_ref[0, hh]
        gate = _nt_dot(qh, kmean)
        sel = _top_mask(gate, past, lane, MOBA_TOPK)
        keep = jnp.where(lane == MOBA_BLK + i, 1.0, sel)
        blk_bias = jnp.where(in_blk, jnp.where(keep > 0.0, 0.0, NEG), 0.0)
        qmoba_ref[0, hh] = (qh.astype(F32) + blk_bias + qaug_ref[hh]).astype(BF16)


def _alibi_slopes_np():
    s = 2.0 ** (-8.0 * np.arange(1, MOBA_HEADS + 1, dtype=np.float64) / MOBA_HEADS)
    for v in s:
        m, _ = np.frexp(v)
        assert m == 0.5, "ALiBi slopes are expected to be powers of two"
    return s.astype(np.float32)


def _moba_consts(sp):
    nblk = sp // MOBA_BLOCK
    assert nblk <= MOBA_ALI - MOBA_BLK
    pos = np.arange(sp)
    kaug = np.zeros((sp, LANES), np.float32)
    kaug[pos, MOBA_BLK + pos // MOBA_BLOCK] = 1.0
    kaug[:, MOBA_ALI] = pos // 64
    kaug[:, MOBA_ALI + 1] = pos % 64
    kaug[:, MOBA_ALI + 2:MOBA_ALI + 5] = 1.0
    slopes = _alibi_slopes_np()
    qaug = np.zeros((MOBA_HEADS, sp, LANES), np.float32)
    for hh in range(MOBA_HEADS):
        qaug[hh, :, MOBA_ALI] = 64.0 * slopes[hh]
        qaug[hh, :, MOBA_ALI + 1] = slopes[hh]
    qaug = jnp.asarray(qaug)
    qterm = -jnp.asarray(slopes)[:, None] * jnp.asarray(pos, F32)[None, :]
    t1, t2, t3 = _split3(qterm)
    for k, tpart in enumerate((t1, t2, t3)):
        qaug = qaug.at[:, :, MOBA_ALI + 2 + k].set(tpart.astype(F32))
    return qaug, jnp.asarray(kaug)


def _moba_prep(qm, kmean_pad, mkv, nb, sp):
    nblk = sp // MOBA_BLOCK
    qaug, kaug = _moba_consts(sp)
    hd = lambda b, i: (b, 0, i, 0)
    return pl.pallas_call(
        _moba_prep_body, grid=(nb, nblk),
        in_specs=[pl.BlockSpec((1, MOBA_HEADS, MOBA_BLOCK, LANES), hd),
                  pl.BlockSpec((1, LANES, LANES), lambda b, i: (b, 0, 0)),
                  pl.BlockSpec((MOBA_BLOCK, LANES), lambda b, i: (b * nblk + i, 0)),
                  pl.BlockSpec((MOBA_HEADS, MOBA_BLOCK, LANES), lambda b, i: (0, i, 0)),
                  pl.BlockSpec((MOBA_BLOCK, LANES), lambda b, i: (i, 0))],
        out_specs=[pl.BlockSpec((1, MOBA_HEADS, MOBA_BLOCK, LANES), hd),
                   pl.BlockSpec((MOBA_BLOCK, LANES), lambda b, i: (b * nblk + i, 0))],
        out_shape=[jax.ShapeDtypeStruct((nb, MOBA_HEADS, sp, LANES), BF16),
                   jax.ShapeDtypeStruct((nb * sp, LANES), BF16)],
        compiler_params=_cparams(2), name="moba_prep",
    )(qm, kmean_pad, mkv, qaug, kaug)


def _flash_body(ii_ref, jj_ref, first_ref, last_ref, mask_ref, q_ref, k_ref, *rest, heads, tq, tk, dv, v_from_k):
    if v_from_k:
        o_ref, m_sc, l_sc, acc_sc = rest
        v_ref = None
    else:
        v_ref, o_ref, m_sc, l_sc, acc_sc = rest
    p = pl.program_id(1)
    rows = heads * tq

    @pl.when(first_ref[p] == 1)
    def _():
        m_sc[...] = jnp.full(m_sc.shape, -jnp.inf, F32)
        l_sc[...] = jnp.zeros_like(l_sc)
        acc_sc[...] = jnp.zeros_like(acc_sc)

    def step(masked):
        q = q_ref[0].reshape(rows, q_ref.shape[-1])
        k = k_ref[0]
        s = _nt_dot(q, k)
        if masked:
            qpos = ii_ref[p] * tq + (lax.broadcasted_iota(jnp.int32, (rows, tk), 0) & (tq - 1))
            kpos = jj_ref[p] * tk + lax.broadcasted_iota(jnp.int32, (rows, tk), 1)
            s = jnp.where(kpos <= qpos, s, NEG)
        m_prev = m_sc[...]
        m_new = jnp.maximum(m_prev, jnp.max(s, axis=-1, keepdims=True))
        alpha = jnp.exp(m_prev - m_new)
        e = jnp.exp(s - m_new)
        l_sc[...] = alpha * l_sc[...] + jnp.sum(e, axis=-1, keepdims=True)
        v = k[:, 0:dv] if v_from_k else v_ref[0]
        acc_sc[...] = alpha * acc_sc[...] + jnp.dot(e.astype(BF16), v, preferred_element_type=F32)
        m_sc[...] = m_new

    @pl.when(mask_ref[p] == 0)
    def _():
        step(False)

    @pl.when(mask_ref[p] == 1)
    def _():
        step(True)

    @pl.when(last_ref[p] == 1)
    def _():
        o_ref[0] = (acc_sc[...] / l_sc[...]).reshape(heads, tq, dv)


def _flash_pairs(nq, tq, tk):
    ii, jj, first, last, mask = [], [], [], [], []
    for i in range(nq):
        jmax = (i * tq + tq - 1) // tk
        for j in range(jmax + 1):
            ii.append(i)
            jj.append(j)
            first.append(int(j == 0))
            last.append(int(j == jmax))
            mask.append(int((j + 1) * tk - 1 > i * tq))
    return [jnp.asarray(np.asarray(a, np.int32)) for a in (ii, jj, first, last, mask)]


def _flash(q, k, v, *, tq, tk, dv):
    nb, heads, sp, dk = q.shape
    assert tq & (tq - 1) == 0
    pairs = _flash_pairs(sp // tq, tq, tk)
    v_from_k = v is None
    qmap = lambda b, p, ii, jj, f, l, m: (b, 0, ii[p], 0)
    kmap = lambda b, p, ii, jj, f, l, m: (b, jj[p], 0)
    in_specs = [pl.BlockSpec((1, heads, tq, dk), qmap), pl.BlockSpec((1, tk, dk), kmap)]
    args = [q, k.reshape(nb, sp, dk)]
    if not v_from_k:
        in_specs.append(pl.BlockSpec((1, tk, dv), kmap))
        args.append(v.reshape(nb, sp, dv))
    grid_spec = pltpu.PrefetchScalarGridSpec(
        num_scalar_prefetch=5, grid=(nb, int(pairs[0].shape[0])), in_specs=in_specs,
        out_specs=pl.BlockSpec((1, heads, tq, dv), qmap),
        scratch_shapes=[pltpu.VMEM((heads * tq, 1), F32), pltpu.VMEM((heads * tq, 1), F32),
                        pltpu.VMEM((heads * tq, dv), F32)])
    body = functools.partial(_flash_body, heads=heads, tq=tq, tk=tk, dv=dv, v_from_k=v_from_k)
    return pl.pallas_call(
        body, grid_spec=grid_spec, out_shape=jax.ShapeDtypeStruct((nb, heads, sp, dv), F32),
        compiler_params=_cparams(2), name="flash_dk%d" % dk,
    )(*pairs, *args)


def _layer_norm(y, g, b):
    mu = jnp.mean(y, axis=-1, keepdims=True)
    d = y - mu
    var = jnp.mean(d * d, axis=-1, keepdims=True)
    return d * lax.rsqrt(var + LN_EPS) * g + b


def _mixout_body(x_ref, omla_ref, ofox_ref, omoba_ref, wuv_ref, wout_ref, g_ref, b_ref, o_ref):
    pieces = []
    for hh in range(MLA_HEADS):
        pieces.append(jnp.dot(omla_ref[0, hh].astype(BF16), wuv_ref[hh], preferred_element_type=F32))
    for hh in range(FOX_HEADS):
        pieces.append(ofox_ref[0, hh])
    for hh in range(MOBA_HEADS):
        pieces.append(omoba_ref[0, hh])
    mix = jnp.concatenate(pieces, axis=-1).astype(BF16)
    y = ALPHA * x_ref[...] + jnp.dot(mix, wout_ref[...], preferred_element_type=F32)
    o_ref[...] = _layer_norm(y, g_ref[...], b_ref[...])


def _mixout(x2d, omla, ofox, omoba, lw, nb, sp, tm):
    nt = sp // tm
    row = lambda i: (i, 0)
    hd = lambda i: (i // nt, 0, i % nt, 0)
    c2 = lambda i: (0, 0)
    return pl.pallas_call(
        _mixout_body, grid=(x2d.shape[0] // tm,),
        in_specs=[pl.BlockSpec((tm, D_MODEL), row),
                  pl.BlockSpec((1, MLA_HEADS, tm, MLA_KV_LORA), hd),
                  pl.BlockSpec((1, FOX_HEADS, tm, HEAD_DIM), hd),
                  pl.BlockSpec((1, MOBA_HEADS, tm, HEAD_DIM), hd),
                  pl.BlockSpec((MLA_HEADS, MLA_KV_LORA, MLA_V), lambda i: (0, 0, 0)),
                  pl.BlockSpec((D_MODEL, D_MODEL), c2),
                  pl.BlockSpec((1, D_MODEL), c2), pl.BlockSpec((1, D_MODEL), c2)],
        out_specs=pl.BlockSpec((tm, D_MODEL), row),
        out_shape=jax.ShapeDtypeStruct(x2d.shape, F32),
        compiler_params=_cparams(1), name="mixout",
    )(x2d, omla, ofox, omoba, lw["wuv"], lw["wout"], lw["ln_mix_g"], lw["ln_mix_b"])


def _ffn_body(x_ref, wg_ref, wu_ref, wd_ref, g_ref, b_ref, o_ref, acc_sc):
    f = pl.program_id(1)

    @pl.when(f == 0)
    def _():
        acc_sc[...] = jnp.zeros_like(acc_sc)

    xb = x_ref[...].astype(BF16)
    gate = jnp.dot(xb, wg_ref[...], preferred_element_type=F32)
    up = jnp.dot(xb, wu_ref[...], preferred_element_type=F32)
    hid = (gate * (1.0 / (1.0 + jnp.exp(-gate))) * up).astype(BF16)
    acc_sc[...] += jnp.dot(hid, wd_ref[...], preferred_element_type=F32)

    @pl.when(f == pl.num_programs(1) - 1)
    def _():
        o_ref[...] = _layer_norm(ALPHA * x_ref[...] + acc_sc[...], g_ref[...], b_ref[...])


def _ffn(x2d, lw, tm, tf):
    row = lambda i, f: (i, 0)
    c2 = lambda i, f: (0, 0)
    return pl.pallas_call(
        _ffn_body, grid=(x2d.shape[0] // tm, D_FF // tf),
        in_specs=[pl.BlockSpec((tm, D_MODEL), row),
                  pl.BlockSpec((D_MODEL, tf), lambda i, f: (0, f)),
                  pl.BlockSpec((D_MODEL, tf), lambda i, f: (0, f)),
                  pl.BlockSpec((tf, D_MODEL), lambda i, f: (f, 0)),
                  pl.BlockSpec((1, D_MODEL), c2), pl.BlockSpec((1, D_MODEL), c2)],
        out_specs=pl.BlockSpec((tm, D_MODEL), row),
        out_shape=jax.ShapeDtypeStruct(x2d.shape, F32),
        scratch_shapes=[pltpu.VMEM((tm, D_MODEL), F32)],
        compiler_params=_cparams(2), name="ffn",
    )(x2d, lw["wgate"], lw["wup"], lw["wdown"], lw["ln_ffn_g"], lw["ln_ffn_b"])


def _page_spec(layer, feat, n_pages, pos_of):
    return pl.BlockSpec((None, None, PAGE, feat), lambda b, c, pt, *_: (layer, pt[b * n_pages + pos_of(b, c)], 0, 0))


def _dec_mla_body(pt_ref, qlat_ref, qpe_ref, nckv_ref, nkpe_ref, *rest, pg):
    ckv_pages, kpe_pages = rest[:pg], rest[pg:2 * pg]
    o_ref, m_sc, l_sc, acc_sc = rest[2 * pg:]
    c = pl.program_id(1)
    qlat = qlat_ref[0]
    qpe = qpe_ref[0]

    @pl.when(c == 0)
    def _():
        kn = nckv_ref[0].astype(BF16).astype(F32)
        kp = nkpe_ref[0].astype(BF16).astype(F32)
        s0 = (jnp.sum(qlat.astype(F32) * kn, axis=-1, keepdims=True)
              + jnp.sum(qpe.astype(F32) * kp, axis=-1, keepdims=True))
        m_sc[...] = s0
        l_sc[...] = jnp.ones_like(l_sc)
        acc_sc[...] = jnp.broadcast_to(kn, acc_sc.shape)

    k = jnp.concatenate([r[...] for r in ckv_pages], axis=0).astype(BF16)
    kp = jnp.concatenate([r[...] for r in kpe_pages], axis=0).astype(BF16)
    s = _nt_dot(qlat, k) + _nt_dot(qpe, kp)
    m_prev = m_sc[...]
    m_new = jnp.maximum(m_prev, jnp.max(s, axis=-1, keepdims=True))
    alpha = jnp.exp(m_prev - m_new)
    e = jnp.exp(s - m_new)
    l_sc[...] = alpha * l_sc[...] + jnp.sum(e, axis=-1, keepdims=True)
    acc_sc[...] = alpha * acc_sc[...] + jnp.dot(e.astype(BF16), k, preferred_element_type=F32)
    m_sc[...] = m_new

    @pl.when(c == pl.num_programs(1) - 1)
    def _():
        o_ref[0] = acc_sc[...] / l_sc[...]


def _dec_mla(pt_flat, qlat, qpe, nckv, nkpe, cache_ckv, cache_kpe, layer, n_pages, pg):
    bd = qlat.shape[0]
    b3 = lambda b, c, pt: (b, 0, 0)
    in_specs = [pl.BlockSpec((1, MLA_HEADS, MLA_KV_LORA), b3), pl.BlockSpec((1, MLA_HEADS, MLA_ROPE), b3),
                pl.BlockSpec((1, 1, MLA_KV_LORA), b3), pl.BlockSpec((1, 1, MLA_ROPE), b3)]
    for i in range(pg):
        in_specs.append(_page_spec(layer, MLA_KV_LORA, n_pages, lambda b, c, i=i: c * pg + i))
    for i in range(pg):
        in_specs.append(_page_spec(layer, MLA_ROPE, n_pages, lambda b, c, i=i: c * pg + i))
    grid_spec = pltpu.PrefetchScalarGridSpec(
        num_scalar_prefetch=1, grid=(bd, n_pages // pg), in_specs=in_specs,
        out_specs=pl.BlockSpec((1, MLA_HEADS, MLA_KV_LORA), b3),
        scratch_shapes=[pltpu.VMEM((MLA_HEADS, 1), F32), pltpu.VMEM((MLA_HEADS, 1), F32),
                        pltpu.VMEM((MLA_HEADS, MLA_KV_LORA), F32)])
    return pl.pallas_call(
        functools.partial(_dec_mla_body, pg=pg), grid_spec=grid_spec,
        out_shape=jax.ShapeDtypeStruct((bd, MLA_HEADS, MLA_KV_LORA), F32),
        compiler_params=_cparams(2), name="dec_mla",
    )(pt_flat, qlat, qpe, nckv, nkpe, *([cache_ckv] * pg), *([cache_kpe] * pg))


def _dec_fox_body(pt_ref, q_ref, nk_ref, nv_ref, nl_ref, su_ref, *rest, pg):
    k_pages, v_pages, l_pages = rest[:pg], rest[pg:2 * pg], rest[2 * pg:3 * pg]
    o_ref, m_sc, l_sc, acc_sc, carry_sc, lf_sc = rest[3 * pg:]
    c = pl.program_id(1)
    t = pg * PAGE
    q = q_ref[0]

    @pl.when(c == 0)
    def _():
        kn = nk_ref[0].astype(BF16).astype(F32)
        m_sc[...] = jnp.sum(q.astype(F32) * kn, axis=-1, keepdims=True)
        l_sc[...] = jnp.ones_like(l_sc)
        acc_sc[...] = jnp.broadcast_to(nv_ref[0].astype(BF16).astype(F32), acc_sc.shape)
        carry_sc[...] = nl_ref[0]
        lf_sc[...] = jnp.zeros_like(lf_sc)

    for i in range(pg):
        lf_sc[i * PAGE:(i + 1) * PAGE, 0:FOX_HEADS] = l_pages[i][...]
    su = su_ref[...]
    l1, l2, l3 = _split3(lf_sc[...])
    suf = (jnp.dot(su, l1, preferred_element_type=F32) + jnp.dot(su, l2, preferred_element_type=F32)
           + jnp.dot(su, l3, preferred_element_type=F32))
    suf_t = suf.T
    carry = carry_sc[...]
    bias = suf_t[0:FOX_HEADS, 0:t] + carry
    carry_sc[...] = carry + suf_t[0:FOX_HEADS, t:t + 1]

    k = jnp.concatenate([r[...] for r in k_pages], axis=0).astype(BF16)
    v = jnp.concatenate([r[...] for r in v_pages], axis=0).astype(BF16)
    s = _nt_dot(q, k) + bias
    m_prev = m_sc[...]
    m_new = jnp.maximum(m_prev, jnp.max(s, axis=-1, keepdims=True))
    alpha = jnp.exp(m_prev - m_new)
    e = jnp.exp(s - m_new)
    l_sc[...] = alpha * l_sc[...] + jnp.sum(e, axis=-1, keepdims=True)
    acc_sc[...] = alpha * acc_sc[...] + jnp.dot(e.astype(BF16), v, preferred_element_type=F32)
    m_sc[...] = m_new

    @pl.when(c == pl.num_programs(1) - 1)
    def _():
        o_ref[0] = acc_sc[...] / l_sc[...]


def _dec_fox(pt_flat, q, nk, nv, nl, cache_k, cache_v, cache_l, layer, n_pages, pg):
    bd = q.shape[0]
    nch = n_pages // pg
    t = pg * PAGE
    r = np.arange(t)
    su = np.concatenate([(r[None, :] > r[:, None]).astype(np.float32), np.ones((LANES, t), np.float32)], axis=0)
    su = jnp.asarray(su, BF16)
    b3 = lambda b, c, pt: (b, 0, 0)
    in_specs = [pl.BlockSpec((1, FOX_HEADS, HEAD_DIM), b3), pl.BlockSpec((1, 1, HEAD_DIM), b3),
                pl.BlockSpec((1, 1, HEAD_DIM), b3), pl.BlockSpec((1, FOX_HEADS, 1), b3),
                pl.BlockSpec((t + LANES, t), lambda b, c, pt: (0, 0))]
    for feat in (HEAD_DIM, HEAD_DIM, FOX_HEADS):
        for i in range(pg):
            in_specs.append(_page_spec(layer, feat, n_pages, lambda b, c, i=i: (nch - 1 - c) * pg + i))
    grid_spec = pltpu.PrefetchScalarGridSpec(
        num_scalar_prefetch=1, grid=(bd, nch), in_specs=in_specs,
        out_specs=pl.BlockSpec((1, FOX_HEADS, HEAD_DIM), b3),
        scratch_shapes=[pltpu.VMEM((FOX_HEADS, 1), F32), pltpu.VMEM((FOX_HEADS, 1), F32),
                        pltpu.VMEM((FOX_HEADS, HEAD_DIM), F32), pltpu.VMEM((FOX_HEADS, 1), F32),
                        pltpu.VMEM((t, LANES), F32)])
    return pl.pallas_call(
        functools.partial(_dec_fox_body, pg=pg), grid_spec=grid_spec,
        out_shape=jax.ShapeDtypeStruct((bd, FOX_HEADS, HEAD_DIM), F32),
        compiler_params=_cparams(2), name="dec_fox",
    )(pt_flat, q, nk, nv, nl, su, *([cache_k] * pg), *([cache_v] * pg), *([cache_l] * pg))


def _dec_moba_sel_body(pt_ref, q_ref, *rest, pg):
    k_pages = rest[:pg]
    o_ref, means_sc = rest[pg:]
    c = pl.program_id(1)
    ppb = MOBA_BLOCK // PAGE
    nb_step = pg // ppb
    rows = []
    for n in range(nb_step):
        tot = jnp.sum(k_pages[ppb * n][...], axis=0, keepdims=True)
        for j in range(1, ppb):
            tot = tot + jnp.sum(k_pages[ppb * n + j][...], axis=0, keepdims=True)
        rows.append(tot * (1.0 / MOBA_BLOCK))
    means_sc[pl.ds(pl.multiple_of(c * nb_step, nb_step), nb_step), :] = jnp.concatenate(rows, axis=0)

    @pl.when(c == pl.num_programs(1) - 1)
    def _():
        gate = _nt_dot(q_ref[0], means_sc[...].astype(BF16))
        lane = lax.broadcasted_iota(jnp.int32, gate.shape, 1)
        out_lane = lax.broadcasted_iota(jnp.int32, (MOBA_HEADS, LANES), 1)
        res = jnp.zeros((MOBA_HEADS, LANES), jnp.int32)
        g = gate
        for r in range(MOBA_TOPK):
            mx = jnp.max(g, axis=-1, keepdims=True)
            idx = jnp.min(jnp.where(g == mx, lane, 1 << 20), axis=-1, keepdims=True)
            res = jnp.where(out_lane == r, idx, res)
            g = jnp.where(lane == idx, -jnp.inf, g)
        o_ref[0] = res


def _dec_moba_sel(pt_flat, q, cache_k, layer, n_pages, pg):
    bd = q.shape[0]
    nblk = n_pages * PAGE // MOBA_BLOCK
    assert nblk >= MOBA_TOPK
    b3 = lambda b, c, pt: (b, 0, 0)
    in_specs = [pl.BlockSpec((1, MOBA_HEADS, HEAD_DIM), b3)]
    for i in range(pg):
        in_specs.append(_page_spec(layer, HEAD_DIM, n_pages, lambda b, c, i=i: c * pg + i))
    grid_spec = pltpu.PrefetchScalarGridSpec(
        num_scalar_prefetch=1, grid=(bd, n_pages // pg), in_specs=in_specs,
        out_specs=pl.BlockSpec((1, MOBA_HEADS, LANES), b3),
        scratch_shapes=[pltpu.VMEM((nblk, HEAD_DIM), F32)])
    return pl.pallas_call(
        functools.partial(_dec_moba_sel_body, pg=pg), grid_spec=grid_spec,
        out_shape=jax.ShapeDtypeStruct((bd, MOBA_HEADS, LANES), jnp.int32),
        compiler_params=_cparams(2), name="dec_moba_sel",
    )(pt_flat, q, *([cache_k] * pg))


def _dec_moba_att_body(pt_ref, sel_ref, q_ref, nk_ref, nv_ref, slope_ref, *rest, past):
    ppb = MOBA_BLOCK // PAGE
    k_pages, v_pages = rest[:ppb], rest[ppb:2 * ppb]
    o_ref, m_sc, l_sc, acc_sc = rest[2 * ppb:]
    b = pl.program_id(0)
    j = pl.program_id(1)
    r = j % MOBA_TOPK
    q = q_ref[0]

    @pl.when(r == 0)
    def _():
        kn = nk_ref[0].astype(BF16).astype(F32)
        m_sc[...] = jnp.sum(q.astype(F32) * kn, axis=-1, keepdims=True)
        l_sc[...] = jnp.ones_like(l_sc)
        acc_sc[...] = nv_ref[0].astype(BF16).astype(F32)

    blk = sel_ref[b * (MOBA_HEADS * MOBA_TOPK) + j]
    k = jnp.concatenate([p[...] for p in k_pages], axis=0).astype(BF16)
    v = jnp.concatenate([p[...] for p in v_pages], axis=0).astype(BF16)
    off = lax.broadcasted_iota(jnp.int32, (1, MOBA_BLOCK), 1).astype(F32)
    dist = (past - blk * MOBA_BLOCK).astype(F32) - off
    s = _nt_dot(q, k) - slope_ref[0] * dist
    m_prev = m_sc[...]
    m_new = jnp.maximum(m_prev, jnp.max(s, axis=-1, keepdims=True))
    alpha = jnp.exp(m_prev - m_new)
    e = jnp.exp(s - m_new)
    l_sc[...] = alpha * l_sc[...] + jnp.sum(e, axis=-1, keepdims=True)
    acc_sc[...] = alpha * acc_sc[...] + jnp.dot(e.astype(BF16), v, preferred_element_type=F32)
    m_sc[...] = m_new

    @pl.when(r == MOBA_TOPK - 1)
    def _():
        o_ref[0] = acc_sc[...] / l_sc[...]


def _dec_moba_att(pt_flat, sel_flat, q, nk, nv, cache_k, cache_v, layer, n_pages):
    bd = q.shape[0]
    ppb = MOBA_BLOCK // PAGE
    per_b = MOBA_HEADS * MOBA_TOPK
    slopes = jnp.broadcast_to(jnp.asarray(_alibi_slopes_np())[:, None, None], (MOBA_HEADS, 1, MOBA_BLOCK))
    qrow = lambda b, j, pt, sel: (b * MOBA_HEADS + j // MOBA_TOPK, 0, 0)
    b3 = lambda b, j, pt, sel: (b, 0, 0)
    in_specs = [pl.BlockSpec((1, 1, HEAD_DIM), qrow), pl.BlockSpec((1, 1, HEAD_DIM), b3),
                pl.BlockSpec((1, 1, HEAD_DIM), b3),
                pl.BlockSpec((1, 1, MOBA_BLOCK), lambda b, j, pt, sel: (j // MOBA_TOPK, 0, 0))]
    for _ in range(2):
        for i in range(ppb):
            in_specs.append(pl.BlockSpec(
                (None, None, PAGE, HEAD_DIM),
                lambda b, j, pt, sel, i=i: (layer, pt[b * n_pages + sel[b * per_b + j] * ppb + i], 0, 0)))
    grid_spec = pltpu.PrefetchScalarGridSpec(
        num_scalar_prefetch=2, grid=(bd, per_b), in_specs=in_specs,
        out_specs=pl.BlockSpec((1, 1, HEAD_DIM), qrow),
        scratch_shapes=[pltpu.VMEM((1, 1), F32), pltpu.VMEM((1, 1), F32), pltpu.VMEM((1, HEAD_DIM), F32)])
    return pl.pallas_call(
        functools.partial(_dec_moba_att_body, past=n_pages * PAGE), grid_spec=grid_spec,
        out_shape=jax.ShapeDtypeStruct((bd * MOBA_HEADS, 1, HEAD_DIM), F32),
        compiler_params=_cparams(2), name="dec_moba_att",
    )(pt_flat, sel_flat, q.reshape(bd * MOBA_HEADS, 1, HEAD_DIM), nk, nv, slopes,
      *([cache_k] * ppb), *([cache_v] * ppb))


def _swap_halves(w):
    half = w.shape[-1] // 2
    return jnp.concatenate([w[..., half:], w[..., :half]], axis=-1)


def _layer_weights(l, w_in, b_forget, mla_q_norm, w_q_up, mla_kv_norm, w_uk, w_uv, w_out, ln_mix_g, ln_mix_b,
                   w_gate, w_up, w_down, ln_ffn_g, ln_ffn_b):
    w = w_in[l]
    splits = (MLA_Q_LORA, MLA_KV_LORA, MLA_ROPE, FOX_HEADS * HEAD_DIM, HEAD_DIM, HEAD_DIM, FOX_HEADS,
              MOBA_HEADS * HEAD_DIM, HEAD_DIM, HEAD_DIM)
    offs = np.cumsum((0,) + splits)
    cq, ckv, kpe, fq, fk, fv, fg, mq, mk, mv = [w[:, offs[i]:offs[i + 1]] for i in range(len(splits))]
    zeros = lambda n: jnp.zeros((D_MODEL, n), w.dtype)

    def head_chunks(wq, heads):
        return jnp.concatenate([jnp.concatenate([wq[:, HEAD_DIM * hh:HEAD_DIM * (hh + 1)], zeros(LANES - HEAD_DIM)], 1)
                                for hh in range(heads)], axis=1)

    wall = jnp.concatenate([cq, ckv, jnp.tile(kpe, (1, 4)), jnp.tile(_swap_halves(kpe), (1, 4)),
                            head_chunks(fq, FOX_HEADS), head_chunks(mq, MOBA_HEADS),
                            fk, fv, mk, mv, fg, zeros(LANES - FOX_HEADS)], axis=1)
    assert wall.shape[1] == N_COLS
    wq = w_q_up[l]
    q_nope = wq[:, :, :MLA_NOPE].reshape(MLA_Q_LORA, MLA_HEADS * MLA_NOPE)
    q_pe = wq[:, :, MLA_NOPE:]
    wq2 = jnp.concatenate([q_nope, q_pe.reshape(MLA_Q_LORA, -1), _swap_halves(q_pe).reshape(MLA_Q_LORA, -1)], axis=1)
    wuk_t = jnp.transpose(w_uk[l], (1, 2, 0))
    zpad = jnp.zeros_like(wuk_t)
    even = jnp.concatenate([wuk_t, zpad], axis=1)
    odd = jnp.concatenate([zpad, wuk_t], axis=1)
    is_odd = (jnp.arange(MLA_HEADS) % 2 == 1)[:, None, None]
    wuk = jnp.where(is_odd, odd, even)
    bfg = jnp.zeros((1, LANES), F32).at[0, :FOX_HEADS].set(b_forget[l])
    return dict(
        wall=wall.astype(BF16), gq=mla_q_norm[l][None, :], gkv=mla_kv_norm[l][None, :], wq2=wq2.astype(BF16),
        wuk=wuk.astype(BF16), bfg=bfg, wuv=jnp.transpose(w_uv[l], (1, 0, 2)).astype(BF16),
        wout=w_out[l].astype(BF16), ln_mix_g=ln_mix_g[l][None, :], ln_mix_b=ln_mix_b[l][None, :],
        wgate=w_gate[l].astype(BF16), wup=w_up[l].astype(BF16), wdown=w_down[l].astype(BF16),
        ln_ffn_g=ln_ffn_g[l][None, :], ln_ffn_b=ln_ffn_b[l][None, :])


def _rope_tables(pos):
    half = MLA_ROPE // 2
    inv = ROPE_THETA ** (-jnp.arange(half, dtype=F32) / half)
    ang = pos.astype(F32)[:, None] * inv[None, :]
    cos = jnp.concatenate([jnp.cos(ang), jnp.cos(ang)], axis=-1)
    sin = jnp.concatenate([-jnp.sin(ang), jnp.sin(ang)], axis=-1)
    return (jnp.tile(cos, (1, 4)), jnp.tile(sin, (1, 4)), jnp.tile(cos, (1, MLA_HEADS)), jnp.tile(sin, (1, MLA_HEADS)))


def _pick(n, prefs):
    for p in prefs:
        if n % p == 0:
            return p
    raise ValueError("no tile for %d" % n)


def kernel(x_prompt, x_sample, cache_mla_ckv, cache_mla_kpe, cache_fox_k, cache_fox_v, cache_fox_logf, cache_moba_k,
           cache_moba_v, page_table, w_in, b_forget, mla_q_norm, w_q_up, mla_kv_norm, w_uk, w_uv, w_out, ln_mix_g,
           ln_mix_b, w_gate, w_up, w_down, ln_ffn_g, ln_ffn_b):
    nb, sp, _ = x_prompt.shape
    bd, dec_t, _ = x_sample.shape
    assert dec_t == 1 and sp % MOBA_BLOCK == 0
    n_pages = page_table.shape[1]
    past = n_pages * PAGE
    pt_flat = page_table.reshape(-1).astype(jnp.int32)
    nblk = sp // MOBA_BLOCK

    tabs_p = _rope_tables(jnp.arange(sp, dtype=jnp.int32))
    tabs_s = _rope_tables(jnp.full((bd,), past, jnp.int32))
    tm_p = _pick(sp, (512, 256))
    tq = 256
    tk = _pick(sp, (512, 256))
    pg = _pick(n_pages, (8, 4, 2))
    pg_sel = _pick(n_pages, (16, 8, 4, 2))

    xp = x_prompt.reshape(nb * sp, D_MODEL)
    xs = x_sample.reshape(bd, D_MODEL)
    new_p = [[] for _ in range(7)]
    new_s = [[] for _ in range(7)]

    for l in range(DEPTH):
        lw = _layer_weights(l, w_in, b_forget, mla_q_norm, w_q_up, mla_kv_norm, w_uk, w_uv, w_out, ln_mix_g,
                            ln_mix_b, w_gate, w_up, w_down, ln_ffn_g, ln_ffn_b)

        qmla, kmla, ckv, kpe, qf, qm, fkv, mkv, logf = _proj(xp, tabs_p, lw, nb, sp, tm_p)
        qfox, kfox = _fox_prep(logf, qf, fkv, nb, sp, tm_p)
        means = _moba_means(mkv, nb, nblk)[:, :, 0, :HEAD_DIM]
        kmean_pad = jnp.zeros((nb, LANES, LANES), F32).at[:, MOBA_BLK:MOBA_BLK + nblk, :HEAD_DIM].set(means)
        qmoba, kmoba = _moba_prep(qm, kmean_pad, mkv, nb, sp)
        fk, fv = fkv[:, :HEAD_DIM], fkv[:, HEAD_DIM:]
        mk, mv = mkv[:, :HEAD_DIM], mkv[:, HEAD_DIM:]
        o_mla = _flash(qmla, kmla, None, tq=tq, tk=tk, dv=MLA_KV_LORA)
        o_fox = _flash(qfox, kfox, fv.astype(BF16), tq=tq, tk=tk, dv=HEAD_DIM)
        o_moba = _flash(qmoba, kmoba, mv.astype(BF16), tq=tq, tk=tk, dv=HEAD_DIM)
        x1 = _mixout(xp, o_mla, o_fox, o_moba, lw, nb, sp, tq)
        xp = _ffn(x1, lw, tm_p, D_FF // 2)
        for lst, a, shp in zip(new_p, (ckv, kpe[:, :MLA_ROPE], fk, fv, logf[:, :FOX_HEADS], mk, mv),
                               (MLA_KV_LORA, MLA_ROPE, HEAD_DIM, HEAD_DIM, FOX_HEADS, HEAD_DIM, HEAD_DIM)):
            lst.append(a.reshape(nb, sp, shp))

        qmla, _, sckv, skpe, qf, qm, sfkv, smkv, slogf = _proj(xs, tabs_s, lw, 1, bd, bd)
        skpe = skpe[:, :MLA_ROPE]
        sfk, sfv = sfkv[:, :HEAD_DIM], sfkv[:, HEAD_DIM:]
        smk, smv = smkv[:, :HEAD_DIM], smkv[:, HEAD_DIM:]
        slogf = slogf[:, :FOX_HEADS]
        qs = jnp.transpose(qmla[0], (1, 0, 2))
        qlat_s = qs[:, :, :MLA_KV_LORA]
        qpe_s = qs[:, :, MLA_KV_LORA:].reshape(bd, MLA_HEADS, 4, MLA_ROPE).sum(axis=2)
        fq_s = jnp.transpose(qf[0, :, :, :HEAD_DIM], (1, 0, 2))
        mq_s = jnp.transpose(qm[0, :, :, :HEAD_DIM], (1, 0, 2))
        o_mla_s = _dec_mla(pt_flat, qlat_s, qpe_s, sckv[:, None, :], skpe[:, None, :], cache_mla_ckv, cache_mla_kpe,
                           l, n_pages, pg)
        o_fox_s = _dec_fox(pt_flat, fq_s, sfk[:, None, :], sfv[:, None, :], slogf[:, :, None], cache_fox_k,
                           cache_fox_v, cache_fox_logf, l, n_pages, pg)
        sel = _dec_moba_sel(pt_flat, mq_s, cache_moba_k, l, n_pages, pg_sel)[:, :, :MOBA_TOPK]
        o_moba_s = _dec_moba_att(pt_flat, sel.reshape(-1), mq_s, smk[:, None, :], smv[:, None, :], cache_moba_k,
                                 cache_moba_v, l, n_pages).reshape(bd, MOBA_HEADS, HEAD_DIM)
        to_heads = lambda o: jnp.transpose(o, (1, 0, 2))[None]
        x1s = _mixout(xs, to_heads(o_mla_s), to_heads(o_fox_s), to_heads(o_moba_s), lw, 1, bd, bd)
        xs = _ffn(x1s, lw, bd, D_FF // 2)
        for lst, a in zip(new_s, (sckv, skpe, sfk, sfv, slogf, smk, smv)):
            lst.append(a.reshape(bd, 1, a.shape[-1]))

    outs = [xp.reshape(nb, sp, D_MODEL), xs.reshape(bd, 1, D_MODEL)]
    for a, b in zip(new_p, new_s):
        outs.append(jnp.stack(a, 0))
        outs.append(jnp.stack(b, 0))
    return tuple(outs)
```

```python
import functools

import numpy as np
import jax
import jax.numpy as jnp
from jax import lax
from jax.experimental import pallas as pl
from jax.experimental.pallas import tpu as pltpu

F32 = jnp.float32
BF16 = jnp.bfloat16

D_MODEL = 1024
HEAD_DIM = 64
MLA_HEADS = 8
MLA_Q_LORA = 384
MLA_KV_LORA = 256
MLA_NOPE = 64
MLA_ROPE = 32
MLA_V = 64
ROPE_THETA = 10000.0
FOX_HEADS = 4
MOBA_HEADS = 4
MOBA_BLOCK = 256
MOBA_TOPK = 3
PAGE = 128
DEPTH = 2
D_FF = 2816
ALPHA = (2.0 * DEPTH) ** 0.25
LN_EPS = 1e-5
RMS_EPS = 1e-6
MLA_SCALE = (MLA_NOPE + MLA_ROPE) ** -0.5
ATT_SCALE = HEAD_DIM ** -0.5

LANES = 128
NEG = -1e30
MLA_GROUPS = 4
NARROW_GROUPS = 2
VMEM_LIMIT = 56 * 1024 * 1024

C_CQ = 0
C_CKV = 384
C_KPE = 640
C_KPE_SW = 768
C_FQ = 896
C_MQ = 1408
C_FKV = 1920
C_MKV = 2048
C_FG = 2176
N_COLS = 2304

FOX_NEGC = 64
FOX_POSC = 76
MOBA_BLK = 64
MOBA_ALI = 96


def _cparams(n_axes):
    return pltpu.CompilerParams(dimension_semantics=("arbitrary",) * n_axes, vmem_limit_bytes=VMEM_LIMIT)


def _split3(x):
    a = x.astype(BF16)
    r = x - a.astype(F32)
    b = r.astype(BF16)
    c = (r - b.astype(F32)).astype(BF16)
    return a, b, c


def _nt_dot(a, b):
    return lax.dot_general(a, b, (((1,), (1,)), ((), ())), preferred_element_type=F32)


def _proj_body(x_ref, cosk_ref, sink_ref, cosq_ref, sinq_ref, wall_ref, gq_ref, gkv_ref, wq2_ref, wuk_ref, bfg_ref,
               qmla_ref, kmla_ref, ckv_ref, kpe_ref, qf_ref, qm_ref, fkv_ref, mkv_ref, logf_ref):
    tm = x_ref.shape[0]
    h = jnp.dot(x_ref[...].astype(BF16), wall_ref[...], preferred_element_type=F32)
    lane = lax.broadcasted_iota(jnp.int32, (tm, LANES), 1)

    cq = h[:, C_CQ:C_CQ + MLA_Q_LORA]
    cqn = cq * lax.rsqrt(jnp.mean(cq * cq, axis=-1, keepdims=True) + RMS_EPS) * gq_ref[...]
    q = jnp.dot(cqn.astype(BF16), wq2_ref[...], preferred_element_type=F32)
    qpe = (q[:, 512:768] * cosq_ref[...] + q[:, 768:1024] * sinq_ref[...]) * MLA_SCALE
    qn = q[:, 0:512].astype(BF16)
    for hh in range(MLA_HEADS):
        pair = hh // 2
        qlat = jnp.dot(qn[:, LANES * pair:LANES * (pair + 1)], wuk_ref[hh], preferred_element_type=F32)
        qmla_ref[0, hh, :, 0:MLA_KV_LORA] = (qlat * MLA_SCALE).astype(BF16)
        grp, sub = hh // 4, hh % 4
        chunk = qpe[:, LANES * grp:LANES * (grp + 1)]
        chunk = jnp.where((lane >> 5) == sub, chunk, 0.0)
        qmla_ref[0, hh, :, MLA_KV_LORA:MLA_KV_LORA + LANES] = chunk.astype(BF16)

    ckv = h[:, C_CKV:C_CKV + MLA_KV_LORA]
    ckvn = ckv * lax.rsqrt(jnp.mean(ckv * ckv, axis=-1, keepdims=True) + RMS_EPS) * gkv_ref[...]
    ckv_ref[...] = ckvn
    kpe = h[:, C_KPE:C_KPE + LANES] * cosk_ref[...] + h[:, C_KPE_SW:C_KPE_SW + LANES] * sink_ref[...]
    kpe_ref[...] = kpe
    kmla_ref[:, 0:MLA_KV_LORA] = ckvn.astype(BF16)
    kmla_ref[:, MLA_KV_LORA:MLA_KV_LORA + LANES] = kpe.astype(BF16)

    for hh in range(FOX_HEADS):
        qf_ref[0, hh] = (h[:, C_FQ + LANES * hh:C_FQ + LANES * (hh + 1)] * ATT_SCALE).astype(BF16)
    for hh in range(MOBA_HEADS):
        qm_ref[0, hh] = (h[:, C_MQ + LANES * hh:C_MQ + LANES * (hh + 1)] * ATT_SCALE).astype(BF16)
    fkv_ref[...] = h[:, C_FKV:C_FKV + LANES]
    mkv_ref[...] = h[:, C_MKV:C_MKV + LANES]

    z = h[:, C_FG:C_FG + LANES] + bfg_ref[...]
    logsig = jnp.minimum(z, 0.0) - jnp.log(1.0 + jnp.exp(-jnp.abs(z)))
    logf_ref[...] = jnp.where(lane < FOX_HEADS, logsig, 0.0)


def _proj(x2d, tabs, lw, nb, sp, tm):
    n = x2d.shape[0]
    nt = sp // tm
    cosk, sink, cosq, sinq = tabs
    row = lambda i: (i, 0)
    tab = lambda i: (i % nt, 0)
    c2 = lambda i: (0, 0)
    c3 = lambda i: (0, 0, 0)
    hd = lambda i: (i // nt, 0, i % nt, 0)
    in_specs = [
        pl.BlockSpec((tm, D_MODEL), row),
        pl.BlockSpec((tm, LANES), tab), pl.BlockSpec((tm, LANES), tab),
        pl.BlockSpec((tm, 256), tab), pl.BlockSpec((tm, 256), tab),
        pl.BlockSpec((D_MODEL, N_COLS), c2),
        pl.BlockSpec((1, MLA_Q_LORA), c2), pl.BlockSpec((1, MLA_KV_LORA), c2),
        pl.BlockSpec((MLA_Q_LORA, 1024), c2),
        pl.BlockSpec((MLA_HEADS, LANES, MLA_KV_LORA), c3),
        pl.BlockSpec((1, LANES), c2),
    ]
    dk = MLA_KV_LORA + LANES
    out_shape = [
        jax.ShapeDtypeStruct((nb, MLA_HEADS, sp, dk), BF16),
        jax.ShapeDtypeStruct((n, dk), BF16),
        jax.ShapeDtypeStruct((n, MLA_KV_LORA), F32),
        jax.ShapeDtypeStruct((n, LANES), F32),
        jax.ShapeDtypeStruct((nb, FOX_HEADS, sp, LANES), BF16),
        jax.ShapeDtypeStruct((nb, MOBA_HEADS, sp, LANES), BF16),
        jax.ShapeDtypeStruct((n, LANES), F32),
        jax.ShapeDtypeStruct((n, LANES), F32),
        jax.ShapeDtypeStruct((n, LANES), F32),
    ]
    out_specs = [
        pl.BlockSpec((1, MLA_HEADS, tm, dk), hd),
        pl.BlockSpec((tm, dk), row),
        pl.BlockSpec((tm, MLA_KV_LORA), row),
        pl.BlockSpec((tm, LANES), row),
        pl.BlockSpec((1, FOX_HEADS, tm, LANES), hd),
        pl.BlockSpec((1, MOBA_HEADS, tm, LANES), hd),
        pl.BlockSpec((tm, LANES), row),
        pl.BlockSpec((tm, LANES), row),
        pl.BlockSpec((tm, LANES), row),
    ]
    return pl.pallas_call(
        _proj_body, grid=(n // tm,), in_specs=in_specs, out_specs=out_specs, out_shape=out_shape,
        compiler_params=_cparams(1), name="proj",
    )(x2d, cosk, sink, cosq, sinq, lw["wall"], lw["gq"], lw["gkv"], lw["wq2"], lw["wuk"], lw["bfg"])


def _fox_prep_body(logf_ref, qf_ref, fkv_ref, tri_ref, pq_ref, pk_ref, qc_ref, kc_ref, qfox_ref, kfox_ref, carry_sc):
    t = pl.program_id(1)

    @pl.when(t == 0)
    def _():
        carry_sc[...] = jnp.zeros_like(carry_sc)

    tm = logf_ref.shape[0]
    tri = tri_ref[...]
    l1, l2, l3 = _split3(logf_ref[...])
    cum = (jnp.dot(tri, l1, preferred_element_type=F32) + jnp.dot(tri, l2, preferred_element_type=F32)
           + jnp.dot(tri, l3, preferred_element_type=F32)) + carry_sc[...]
    carry_sc[...] = cum[tm - 1:tm, :]
    c1, c2, c3 = _split3(cum)
    parts = jnp.concatenate([c1, c2, c3], axis=1)
    lane = lax.broadcasted_iota(jnp.int32, (tm, LANES), 1)
    kaug = jnp.dot(parts, pk_ref[...], preferred_element_type=F32) + kc_ref[...]
    kfox_ref[...] = (jnp.where(lane < HEAD_DIM, fkv_ref[...], 0.0) + kaug).astype(BF16)
    for hh in range(FOX_HEADS):
        qaug = jnp.dot(parts, pq_ref[hh], preferred_element_type=F32) + qc_ref[hh]
        qfox_ref[0, hh] = (qf_ref[0, hh].astype(F32) + qaug).astype(BF16)


def _fox_consts(tm):
    r = np.arange(tm)
    tri = (r[None, :] <= r[:, None]).astype(np.float32)
    pk = np.zeros((3 * LANES, LANES), np.float32)
    pq = np.zeros((FOX_HEADS, 3 * LANES, LANES), np.float32)
    kc = np.zeros((1, LANES), np.float32)
    qc = np.zeros((FOX_HEADS, 1, LANES), np.float32)
    for part in range(3):
        kc[0, FOX_POSC + part] = 1.0
        for hh in range(FOX_HEADS):
            pk[part * LANES + hh, FOX_NEGC + 4 * part + hh] = -1.0
            pq[hh, part * LANES + hh, FOX_POSC + part] = 1.0
            qc[hh, 0, FOX_NEGC + 4 * part + hh] = 1.0
    return (jnp.asarray(tri, BF16), jnp.asarray(pq, BF16), jnp.asarray(pk, BF16), jnp.asarray(qc), jnp.asarray(kc))


def _fox_prep(logf, qf, fkv, nb, sp, tm):
    nt = sp // tm
    tri, pq, pk, qc, kc = _fox_consts(tm)
    row = lambda b, t: (b * nt + t, 0)
    hd = lambda b, t: (b, 0, t, 0)
    c2 = lambda b, t: (0, 0)
    c3 = lambda b, t: (0, 0, 0)
    return pl.pallas_call(
        _fox_prep_body, grid=(nb, nt),
        in_specs=[pl.BlockSpec((tm, LANES), row), pl.BlockSpec((1, FOX_HEADS, tm, LANES), hd),
                  pl.BlockSpec((tm, LANES), row), pl.BlockSpec((tm, tm), c2),
                  pl.BlockSpec((FOX_HEADS, 3 * LANES, LANES), c3), pl.BlockSpec((3 * LANES, LANES), c2),
                  pl.BlockSpec((FOX_HEADS, 1, LANES), c3), pl.BlockSpec((1, LANES), c2)],
        out_specs=[pl.BlockSpec((1, FOX_HEADS, tm, LANES), hd), pl.BlockSpec((tm, LANES), row)],
        out_shape=[jax.ShapeDtypeStruct((nb, FOX_HEADS, sp, LANES), BF16),
                   jax.ShapeDtypeStruct((nb * sp, LANES), BF16)],
        scratch_shapes=[pltpu.VMEM((1, LANES), F32)],
        compiler_params=_cparams(2), name="fox_prep",
    )(logf, qf, fkv, tri, pq, pk, qc, kc)


def _moba_means_body(mkv_ref, o_ref):
    o_ref[0, 0] = jnp.sum(mkv_ref[...], axis=0, keepdims=True) * (1.0 / MOBA_BLOCK)


def _moba_means(mkv, nb, nblk):
    return pl.pallas_call(
        _moba_means_body, grid=(nb, nblk),
        in_specs=[pl.BlockSpec((MOBA_BLOCK, LANES), lambda b, n: (b * nblk + n, 0))],
        out_specs=pl.BlockSpec((1, 1, 1, LANES), lambda b, n: (b, n, 0, 0)),
        out_shape=jax.ShapeDtypeStruct((nb, nblk, 1, LANES), F32),
        compiler_params=_cparams(2), name="moba_means",
    )(mkv)


def _top_mask(gate, valid, lane, picks):
    g = jnp.where(valid, gate, -jnp.inf)
    sel = jnp.zeros(gate.shape, F32)
    for _ in range(picks):
        mx = jnp.max(g, axis=-1, keepdims=True)
        idx = jnp.min(jnp.where(g == mx, lane, 1 << 20), axis=-1, keepdims=True)
        pick = lane == idx
        sel = jnp.where(pick, jnp.where(valid, 1.0, sel), sel)
        g = jnp.where(pick, -jnp.inf, g)
    return sel


def _moba_prep_body(qm_ref, kmean_ref, mkv_ref, qaug_ref, kaug_ref, qmoba_ref, kmoba_ref):
    i = pl.program_id(1)
    tm = mkv_ref.shape[0]
    lane = lax.broadcasted_iota(jnp.int32, (tm, LANES), 1)
    kmean = kmean_ref[0].astype(BF16)
    past = (lane >= MOBA_BLK) & (lane < MOBA_BLK + i)
    in_blk = (lane >= MOBA_BLK) & (lane < MOBA_ALI)
    kmoba_ref[...] = (jnp.where(lane < HEAD_DIM, mkv_ref[...], 0.0) + kaug_ref[...]).astype(BF16)
    for hh in range(MOBA_HEADS):
        qh = qm_ref[0, hh]
        gate = _nt_dot(qh, kmean)
        sel = _top_mask(gate, past, lane, MOBA_TOPK)
        keep = jnp.where(lane == MOBA_BLK + i, 1.0, sel)
        blk_bias = jnp.where(in_blk, jnp.where(keep > 0.0, 0.0, NEG), 0.0)
        qmoba_ref[0, hh] = (qh.astype(F32) + blk_bias + qaug_ref[hh]).astype(BF16)


def _alibi_slopes_np():
    s = 2.0 ** (-8.0 * np.arange(1, MOBA_HEADS + 1, dtype=np.float64) / MOBA_HEADS)
    for v in s:
        m, _ = np.frexp(v)
        assert m == 0.5, "ALiBi slopes are expected to be powers of two"
    return s.astype(np.float32)


def _moba_consts(sp):
    nblk = sp // MOBA_BLOCK
    assert nblk <= MOBA_ALI - MOBA_BLK
    pos = np.arange(sp)
    kaug = np.zeros((sp, LANES), np.float32)
    kaug[pos, MOBA_BLK + pos // MOBA_BLOCK] = 1.0
    kaug[:, MOBA_ALI] = pos // 64
    kaug[:, MOBA_ALI + 1] = pos % 64
    kaug[:, MOBA_ALI + 2:MOBA_ALI + 5] = 1.0
    slopes = _alibi_slopes_np()
    qaug = np.zeros((MOBA_HEADS, sp, LANES), np.float32)
    for hh in range(MOBA_HEADS):
        qaug[hh, :, MOBA_ALI] = 64.0 * slopes[hh]
        qaug[hh, :, MOBA_ALI + 1] = slopes[hh]
    qaug = jnp.asarray(qaug)
    qterm = -jnp.asarray(slopes)[:, None] * jnp.asarray(pos, F32)[None, :]
    t1, t2, t3 = _split3(qterm)
    for k, tpart in enumerate((t1, t2, t3)):
        qaug = qaug.at[:, :, MOBA_ALI + 2 + k].set(tpart.astype(F32))
    return qaug, jnp.asarray(kaug)


def _moba_prep(qm, kmean_pad, mkv, nb, sp):
    nblk = sp // MOBA_BLOCK
    qaug, kaug = _moba_consts(sp)
    hd = lambda b, i: (b, 0, i, 0)
    return pl.pallas_call(
        _moba_prep_body, grid=(nb, nblk),
        in_specs=[pl.BlockSpec((1, MOBA_HEADS, MOBA_BLOCK, LANES), hd),
                  pl.BlockSpec((1, LANES, LANES), lambda b, i: (b, 0, 0)),
                  pl.BlockSpec((MOBA_BLOCK, LANES), lambda b, i: (b * nblk + i, 0)),
                  pl.BlockSpec((MOBA_HEADS, MOBA_BLOCK, LANES), lambda b, i: (0, i, 0)),
                  pl.BlockSpec((MOBA_BLOCK, LANES), lambda b, i: (i, 0))],
        out_specs=[pl.BlockSpec((1, MOBA_HEADS, MOBA_BLOCK, LANES), hd),
                   pl.BlockSpec((MOBA_BLOCK, LANES), lambda b, i: (b * nblk + i, 0))],
        out_shape=[jax.ShapeDtypeStruct((nb, MOBA_HEADS, sp, LANES), BF16),
                   jax.ShapeDtypeStruct((nb * sp, LANES), BF16)],
        compiler_params=_cparams(2), name="moba_prep",
    )(qm, kmean_pad, mkv, qaug, kaug)


def _flash_body(ii_ref, jj_ref, first_ref, last_ref, mask_ref, q_ref, k_ref, *rest, heads, groups, tq, tk, sub, dv,
                v_from_k):
    if v_from_k:
        o_ref, m_sc, l_sc, acc_sc = rest
        v_ref = None
    else:
        v_ref, o_ref, m_sc, l_sc, acc_sc = rest
    p = pl.program_id(1)

    @pl.when(first_ref[p] == 1)
    def _():
        m_sc[...] = jnp.full(m_sc.shape, -jnp.inf, F32)
        l_sc[...] = jnp.zeros_like(l_sc)
        acc_sc[...] = jnp.zeros_like(acc_sc)

    hpg = heads // groups
    rows = hpg * tq

    def step(masked):
        for kc in range(tk // sub):
            k = k_ref[0, kc * sub:(kc + 1) * sub, :]
            v = k[:, 0:dv] if v_from_k else v_ref[0, kc * sub:(kc + 1) * sub, :]
            if masked:
                qpos = ii_ref[p] * tq + (lax.broadcasted_iota(jnp.int32, (rows, sub), 0) & (tq - 1))
                kpos = jj_ref[p] * tk + kc * sub + lax.broadcasted_iota(jnp.int32, (rows, sub), 1)
                keep = kpos <= qpos
            for g in range(groups):
                q = q_ref[0, g * hpg:(g + 1) * hpg].reshape(rows, q_ref.shape[-1])
                s = _nt_dot(q, k)
                if masked:
                    s = jnp.where(keep, s, NEG)
                m_prev = m_sc[g]
                m_new = jnp.maximum(m_prev, jnp.max(s, axis=-1, keepdims=True))
                alpha = jnp.exp(m_prev - m_new)
                e = jnp.exp(s - m_new)
                l_sc[g] = alpha * l_sc[g] + jnp.sum(e, axis=-1, keepdims=True)
                acc_sc[g] = alpha * acc_sc[g] + jnp.dot(e.astype(BF16), v, preferred_element_type=F32)
                m_sc[g] = m_new

    @pl.when(mask_ref[p] == 0)
    def _():
        step(False)

    @pl.when(mask_ref[p] == 1)
    def _():
        step(True)

    @pl.when(last_ref[p] == 1)
    def _():
        for g in range(groups):
            o_ref[0, g * hpg:(g + 1) * hpg] = (acc_sc[g] / l_sc[g]).reshape(hpg, tq, dv)


def _flash_pairs(nq, tq, tk):
    ii, jj, first, last, mask = [], [], [], [], []
    for i in range(nq):
        jmax = (i * tq + tq - 1) // tk
        for j in range(jmax + 1):
            ii.append(i)
            jj.append(j)
            first.append(int(j == 0))
            last.append(int(j == jmax))
            mask.append(int((j + 1) * tk - 1 > i * tq))
    return [jnp.asarray(np.asarray(a, np.int32)) for a in (ii, jj, first, last, mask)]


def _flash(q, k, v, *, tq, tk, dv, groups, sub, name=None):
    nb, heads, sp, dk = q.shape
    assert tq & (tq - 1) == 0 and heads % groups == 0 and tk % sub == 0
    rows = heads // groups * tq
    pairs = _flash_pairs(sp // tq, tq, tk)
    v_from_k = v is None
    qmap = lambda b, p, ii, jj, f, l, m: (b, 0, ii[p], 0)
    kmap = lambda b, p, ii, jj, f, l, m: (b, jj[p], 0)
    in_specs = [pl.BlockSpec((1, heads, tq, dk), qmap), pl.BlockSpec((1, tk, dk), kmap)]
    args = [q, k.reshape(nb, sp, dk)]
    if not v_from_k:
        in_specs.append(pl.BlockSpec((1, tk, dv), kmap))
        args.append(v.reshape(nb, sp, dv))
    grid_spec = pltpu.PrefetchScalarGridSpec(
        num_scalar_prefetch=5, grid=(nb, int(pairs[0].shape[0])), in_specs=in_specs,
        out_specs=pl.BlockSpec((1, heads, tq, dv), qmap),
        scratch_shapes=[pltpu.VMEM((groups, rows, 1), F32), pltpu.VMEM((groups, rows, 1), F32),
                        pltpu.VMEM((groups, rows, dv), F32)])
    body = functools.partial(_flash_body, heads=heads, groups=groups, tq=tq, tk=tk, sub=sub, dv=dv,
                             v_from_k=v_from_k)
    return pl.pallas_call(
        body, grid_spec=grid_spec, out_shape=jax.ShapeDtypeStruct((nb, heads, sp, dv), F32),
        compiler_params=_cparams(2), name=name or "flash_dk%d" % dk,
    )(*pairs, *args)


def _layer_norm(y, g, b):
    mu = jnp.mean(y, axis=-1, keepdims=True)
    d = y - mu
    var = jnp.mean(d * d, axis=-1, keepdims=True)
    return d * lax.rsqrt(var + LN_EPS) * g + b


def _mixout_body(x_ref, omla_ref, ofox_ref, omoba_ref, wuv_ref, wout_ref, g_ref, b_ref, o_ref):
    pieces = []
    for hh in range(MLA_HEADS):
        pieces.append(jnp.dot(omla_ref[0, hh].astype(BF16), wuv_ref[hh], preferred_element_type=F32))
    for hh in range(FOX_HEADS):
        pieces.append(ofox_ref[0, hh])
    for hh in range(MOBA_HEADS):
        pieces.append(omoba_ref[0, hh])
    mix = jnp.concatenate(pieces, axis=-1).astype(BF16)
    y = ALPHA * x_ref[...] + jnp.dot(mix, wout_ref[...], preferred_element_type=F32)
    o_ref[...] = _layer_norm(y, g_ref[...], b_ref[...])


def _mixout(x2d, omla, ofox, omoba, lw, nb, sp, tm):
    nt = sp // tm
    row = lambda i: (i, 0)
    hd = lambda i: (i // nt, 0, i % nt, 0)
    c2 = lambda i: (0, 0)
    return pl.pallas_call(
        _mixout_body, grid=(x2d.shape[0] // tm,),
        in_specs=[pl.BlockSpec((tm, D_MODEL), row),
                  pl.BlockSpec((1, MLA_HEADS, tm, MLA_KV_LORA), hd),
                  pl.BlockSpec((1, FOX_HEADS, tm, HEAD_DIM), hd),
                  pl.BlockSpec((1, MOBA_HEADS, tm, HEAD_DIM), hd),
                  pl.BlockSpec((MLA_HEADS, MLA_KV_LORA, MLA_V), lambda i: (0, 0, 0)),
                  pl.BlockSpec((D_MODEL, D_MODEL), c2),
                  pl.BlockSpec((1, D_MODEL), c2), pl.BlockSpec((1, D_MODEL), c2)],
        out_specs=pl.BlockSpec((tm, D_MODEL), row),
        out_shape=jax.ShapeDtypeStruct(x2d.shape, F32),
        compiler_params=_cparams(1), name="mixout",
    )(x2d, omla, ofox, omoba, lw["wuv"], lw["wout"], lw["ln_mix_g"], lw["ln_mix_b"])


def _ffn_body(x_ref, wg_ref, wu_ref, wd_ref, g_ref, b_ref, o_ref, acc_sc):
    f = pl.program_id(1)

    @pl.when(f == 0)
    def _():
        acc_sc[...] = jnp.zeros_like(acc_sc)

    xb = x_ref[...].astype(BF16)
    gate = jnp.dot(xb, wg_ref[...], preferred_element_type=F32)
    up = jnp.dot(xb, wu_ref[...], preferred_element_type=F32)
    hid = (gate * (1.0 / (1.0 + jnp.exp(-gate))) * up).astype(BF16)
    acc_sc[...] += jnp.dot(hid, wd_ref[...], preferred_element_type=F32)

    @pl.when(f == pl.num_programs(1) - 1)
    def _():
        o_ref[...] = _layer_norm(ALPHA * x_ref[...] + acc_sc[...], g_ref[...], b_ref[...])


def _ffn(x2d, lw, tm, tf):
    row = lambda i, f: (i, 0)
    c2 = lambda i, f: (0, 0)
    return pl.pallas_call(
        _ffn_body, grid=(x2d.shape[0] // tm, D_FF // tf),
        in_specs=[pl.BlockSpec((tm, D_MODEL), row),
                  pl.BlockSpec((D_MODEL, tf), lambda i, f: (0, f)),
                  pl.BlockSpec((D_MODEL, tf), lambda i, f: (0, f)),
                  pl.BlockSpec((tf, D_MODEL), lambda i, f: (f, 0)),
                  pl.BlockSpec((1, D_MODEL), c2), pl.BlockSpec((1, D_MODEL), c2)],
        out_specs=pl.BlockSpec((tm, D_MODEL), row),
        out_shape=jax.ShapeDtypeStruct(x2d.shape, F32),
        scratch_shapes=[pltpu.VMEM((tm, D_MODEL), F32)],
        compiler_params=_cparams(2), name="ffn",
    )(x2d, lw["wgate"], lw["wup"], lw["wdown"], lw["ln_ffn_g"], lw["ln_ffn_b"])


def _page_copies(page_of, caches, bufs, sems, layer, pg, b, chunk, slot):
    out = []
    for ci, (hbm, buf) in enumerate(zip(caches, bufs)):
        for i in range(pg):
            page = 0 if chunk is None else page_of(b, chunk, i)
            out.append(pltpu.make_async_copy(hbm.at[layer, page], buf.at[slot, i], sems.at[ci, slot]))
    return out


def _paged_fetch(page_of, caches, bufs, sems, layer, pg):
    nch = pl.num_programs(1)
    g = pl.program_id(0) * nch + pl.program_id(1)
    slot = g % 2

    @pl.when(g == 0)
    def _():
        for cp in _page_copies(page_of, caches, bufs, sems, layer, pg, pl.program_id(0), pl.program_id(1), slot):
            cp.start()

    @pl.when(g + 1 < pl.num_programs(0) * nch)
    def _():
        nxt = g + 1
        for cp in _page_copies(page_of, caches, bufs, sems, layer, pg, nxt // nch, nxt % nch, 1 - slot):
            cp.start()

    for cp in _page_copies(page_of, caches, bufs, sems, layer, pg, None, None, slot):
        cp.wait()
    return slot


def _chunk_pages(pt_ref, n_pages, pg, chunk_of=lambda c: c):
    return lambda b, c, i: pt_ref[b * n_pages + chunk_of(c) * pg + i]


def _paged_scratch(shapes, pg):
    return ([pltpu.VMEM((2, pg) + tuple(s), F32) for s in shapes]
            + [pltpu.SemaphoreType.DMA((len(shapes), 2))])


def _dec_mla_body(pt_ref, qlat_ref, qpe_ref, nckv_ref, nkpe_ref, ckv_hbm, kpe_hbm, o_ref, m_sc, l_sc, acc_sc,
                  ckv_buf, kpe_buf, sems, *, pg, layer, n_pages):
    c = pl.program_id(1)
    slot = _paged_fetch(_chunk_pages(pt_ref, n_pages, pg), (ckv_hbm, kpe_hbm), (ckv_buf, kpe_buf), sems, layer, pg)
    qlat = qlat_ref[0]
    qpe = qpe_ref[0]

    @pl.when(c == 0)
    def _():
        kn = nckv_ref[0].astype(BF16).astype(F32)
        kp = nkpe_ref[0].astype(BF16).astype(F32)
        s0 = (jnp.sum(qlat.astype(F32) * kn, axis=-1, keepdims=True)
              + jnp.sum(qpe.astype(F32) * kp, axis=-1, keepdims=True))
        m_sc[...] = s0
        l_sc[...] = jnp.ones_like(l_sc)
        acc_sc[...] = jnp.broadcast_to(kn, acc_sc.shape)

    k = ckv_buf[slot].reshape(pg * PAGE, MLA_KV_LORA).astype(BF16)
    kpt = jnp.concatenate([kpe_buf[slot, i] for i in range(pg)], axis=1).astype(BF16)
    s = _nt_dot(qlat, k) + jnp.dot(qpe, kpt, preferred_element_type=F32)
    m_prev = m_sc[...]
    m_new = jnp.maximum(m_prev, jnp.max(s, axis=-1, keepdims=True))
    alpha = jnp.exp(m_prev - m_new)
    e = jnp.exp(s - m_new)
    l_sc[...] = alpha * l_sc[...] + jnp.sum(e, axis=-1, keepdims=True)
    acc_sc[...] = alpha * acc_sc[...] + jnp.dot(e.astype(BF16), k, preferred_element_type=F32)
    m_sc[...] = m_new

    @pl.when(c == pl.num_programs(1) - 1)
    def _():
        o_ref[0] = acc_sc[...] / l_sc[...]


def _dec_mla(pt_flat, qlat, qpe, nckv, nkpe, cache_ckv, cache_kpe, layer, n_pages, pg):
    bd = qlat.shape[0]
    b3 = lambda b, c, pt: (b, 0, 0)
    in_specs = [pl.BlockSpec((1, MLA_HEADS, MLA_KV_LORA), b3), pl.BlockSpec((1, MLA_HEADS, MLA_ROPE), b3),
                pl.BlockSpec((1, 1, MLA_KV_LORA), b3), pl.BlockSpec((1, 1, MLA_ROPE), b3)]
    in_specs += [pl.BlockSpec(memory_space=pl.ANY)] * 2
    grid_spec = pltpu.PrefetchScalarGridSpec(
        num_scalar_prefetch=1, grid=(bd, n_pages // pg), in_specs=in_specs,
        out_specs=pl.BlockSpec((1, MLA_HEADS, MLA_KV_LORA), b3),
        scratch_shapes=[pltpu.VMEM((MLA_HEADS, 1), F32), pltpu.VMEM((MLA_HEADS, 1), F32),
                        pltpu.VMEM((MLA_HEADS, MLA_KV_LORA), F32)]
        + _paged_scratch([(PAGE, MLA_KV_LORA), (MLA_ROPE, PAGE)], pg))
    return pl.pallas_call(
        functools.partial(_dec_mla_body, pg=pg, layer=layer, n_pages=n_pages), grid_spec=grid_spec,
        out_shape=jax.ShapeDtypeStruct((bd, MLA_HEADS, MLA_KV_LORA), F32),
        compiler_params=_cparams(2), name="dec_mla",
    )(pt_flat, qlat, qpe, nckv, nkpe, cache_ckv, cache_kpe)


def _suffix_sums(x, lane):
    for sh in (1, 2, 4, 8, 16, 32, 64):
        x = x + jnp.where(lane < LANES - sh, pltpu.roll(x, LANES - sh, axis=1), 0.0)
    return x


def _dec_fox_body(pt_ref, q_ref, nk_ref, nv_ref, nl_ref, k_hbm, v_hbm, l_hbm, o_ref, m_sc, l_sc, acc_sc, carry_sc,
                  k_buf, v_buf, l_buf, sems, *, pg, layer, n_pages):
    c = pl.program_id(1)
    nch = pl.num_programs(1)
    pages = _chunk_pages(pt_ref, n_pages, pg, lambda ch: nch - 1 - ch)
    slot = _paged_fetch(pages, (k_hbm, v_hbm, l_hbm), (k_buf, v_buf, l_buf), sems, layer, pg)
    q = q_ref[0]

    @pl.when(c == 0)
    def _():
        kn = nk_ref[0].astype(BF16).astype(F32)
        m_sc[...] = jnp.sum(q.astype(F32) * kn, axis=-1, keepdims=True)
        l_sc[...] = jnp.ones_like(l_sc)
        acc_sc[...] = jnp.broadcast_to(nv_ref[0].astype(BF16).astype(F32), acc_sc.shape)
        carry_sc[...] = nl_ref[0]

    logf = jnp.concatenate([l_buf[slot, i] for i in range(pg)], axis=0)
    lane = lax.broadcasted_iota(jnp.int32, logf.shape, 1)
    incl = _suffix_sums(logf, lane)
    strict = incl - logf
    run = carry_sc[...]
    pieces = [None] * pg
    for i in reversed(range(pg)):
        rows = slice(FOX_HEADS * i, FOX_HEADS * (i + 1))
        pieces[i] = strict[rows, :] + run
        run = run + incl[rows, 0:1]
    carry_sc[...] = run
    bias = jnp.concatenate(pieces, axis=1)

    kt = jnp.concatenate([k_buf[slot, i] for i in range(pg)], axis=1).astype(BF16)
    vt = jnp.concatenate([v_buf[slot, i] for i in range(pg)], axis=1).astype(BF16)
    s = jnp.dot(q, kt, preferred_element_type=F32) + bias
    m_prev = m_sc[...]
    m_new = jnp.maximum(m_prev, jnp.max(s, axis=-1, keepdims=True))
    alpha = jnp.exp(m_prev - m_new)
    e = jnp.exp(s - m_new)
    l_sc[...] = alpha * l_sc[...] + jnp.sum(e, axis=-1, keepdims=True)
    acc_sc[...] = alpha * acc_sc[...] + _nt_dot(e.astype(BF16), vt)
    m_sc[...] = m_new

    @pl.when(c == pl.num_programs(1) - 1)
    def _():
        o_ref[0] = acc_sc[...] / l_sc[...]


def _dec_fox(pt_flat, q, nk, nv, nl, cache_k, cache_v, cache_l, layer, n_pages, pg):
    bd = q.shape[0]
    nch = n_pages // pg
    b3 = lambda b, c, pt: (b, 0, 0)
    in_specs = [pl.BlockSpec((1, FOX_HEADS, HEAD_DIM), b3), pl.BlockSpec((1, 1, HEAD_DIM), b3),
                pl.BlockSpec((1, 1, HEAD_DIM), b3), pl.BlockSpec((1, FOX_HEADS, 1), b3)]
    in_specs += [pl.BlockSpec(memory_space=pl.ANY)] * 3
    grid_spec = pltpu.PrefetchScalarGridSpec(
        num_scalar_prefetch=1, grid=(bd, nch), in_specs=in_specs,
        out_specs=pl.BlockSpec((1, FOX_HEADS, HEAD_DIM), b3),
        scratch_shapes=[pltpu.VMEM((FOX_HEADS, 1), F32), pltpu.VMEM((FOX_HEADS, 1), F32),
                        pltpu.VMEM((FOX_HEADS, HEAD_DIM), F32), pltpu.VMEM((FOX_HEADS, 1), F32)]
        + _paged_scratch([(HEAD_DIM, PAGE), (HEAD_DIM, PAGE), (FOX_HEADS, PAGE)], pg))
    return pl.pallas_call(
        functools.partial(_dec_fox_body, pg=pg, layer=layer, n_pages=n_pages), grid_spec=grid_spec,
        out_shape=jax.ShapeDtypeStruct((bd, FOX_HEADS, HEAD_DIM), F32),
        compiler_params=_cparams(2), name="dec_fox",
    )(pt_flat, q, nk, nv, nl, cache_k, cache_v, cache_l)


def _dec_moba_sel_body(pt_ref, q_ref, k_hbm, o_ref, means_sc, k_buf, sems, *, pg, nblk, layer, n_pages):
    c = pl.program_id(1)
    slot = _paged_fetch(_chunk_pages(pt_ref, n_pages, pg), (k_hbm,), (k_buf,), sems, layer, pg)
    ppb = MOBA_BLOCK // PAGE
    nb_step = pg // ppb

    @pl.when(c == 0)
    def _():
        means_sc[...] = jnp.zeros_like(means_sc)

    col_lane = lax.broadcasted_iota(jnp.int32, means_sc.shape, 1)
    means = means_sc[...]
    for n in range(nb_step):
        tot = k_buf[slot, ppb * n]
        for j in range(1, ppb):
            tot = tot + k_buf[slot, ppb * n + j]
        col = jnp.sum(tot, axis=1, keepdims=True) * (1.0 / MOBA_BLOCK)
        means = jnp.where(col_lane == c * nb_step + n, col, means)
    means_sc[...] = means

    @pl.when(c == pl.num_programs(1) - 1)
    def _():
        gate = jnp.dot(q_ref[0], means.astype(BF16), preferred_element_type=F32)
        lane = lax.broadcasted_iota(jnp.int32, gate.shape, 1)
        g = jnp.where(lane < nblk, gate, -jnp.inf)
        res = jnp.zeros((MOBA_HEADS, LANES), jnp.int32)
        for r in range(MOBA_TOPK):
            mx = jnp.max(g, axis=-1, keepdims=True)
            idx = jnp.min(jnp.where(g == mx, lane, 1 << 20), axis=-1, keepdims=True)
            res = jnp.where(lane == r, idx, res)
            g = jnp.where(lane == idx, -jnp.inf, g)
        o_ref[0] = res


def _dec_moba_sel(pt_flat, q, cache_k, layer, n_pages, pg):
    bd = q.shape[0]
    nblk = n_pages * PAGE // MOBA_BLOCK
    assert MOBA_TOPK <= nblk <= LANES
    b3 = lambda b, c, pt: (b, 0, 0)
    in_specs = [pl.BlockSpec((1, MOBA_HEADS, HEAD_DIM), b3), pl.BlockSpec(memory_space=pl.ANY)]
    grid_spec = pltpu.PrefetchScalarGridSpec(
        num_scalar_prefetch=1, grid=(bd, n_pages // pg), in_specs=in_specs,
        out_specs=pl.BlockSpec((1, MOBA_HEADS, LANES), b3),
        scratch_shapes=[pltpu.VMEM((HEAD_DIM, LANES), F32)] + _paged_scratch([(HEAD_DIM, PAGE)], pg))
    return pl.pallas_call(
        functools.partial(_dec_moba_sel_body, pg=pg, nblk=nblk, layer=layer, n_pages=n_pages), grid_spec=grid_spec,
        out_shape=jax.ShapeDtypeStruct((bd, MOBA_HEADS, LANES), jnp.int32),
        compiler_params=_cparams(2), name="dec_moba_sel",
    )(pt_flat, q, cache_k)


def _dec_moba_att_body(pt_ref, sel_ref, q_ref, nk_ref, nv_ref, k_hbm, v_hbm, o_ref, k_buf, v_buf, sems, *, past,
                       layer, n_pages):
    ppb = MOBA_BLOCK // PAGE
    n_sel = MOBA_HEADS * MOBA_TOPK
    b = pl.program_id(0)
    pages = lambda bb, c, i: pt_ref[bb * n_pages + sel_ref[bb * n_sel + i // ppb] * ppb + i % ppb]
    slot = _paged_fetch(pages, (k_hbm, v_hbm), (k_buf, v_buf), sems, layer, n_sel * ppb)
    slopes = _alibi_slopes_np()
    kn = nk_ref[0].astype(BF16).astype(F32)
    vn = nv_ref[0].astype(BF16).astype(F32)
    off = lax.broadcasted_iota(jnp.int32, (1, MOBA_BLOCK), 1).astype(F32)
    outs = []
    for hh in range(MOBA_HEADS):
        q = q_ref[0, hh:hh + 1, :]
        first = hh * MOBA_TOPK * ppb
        kt = jnp.concatenate([k_buf[slot, first + i] for i in range(MOBA_TOPK * ppb)], axis=1).astype(BF16)
        vt = jnp.concatenate([v_buf[slot, first + i] for i in range(MOBA_TOPK * ppb)], axis=1).astype(BF16)
        dist = jnp.concatenate(
            [(past - sel_ref[b * n_sel + hh * MOBA_TOPK + r] * MOBA_BLOCK).astype(F32) - off
             for r in range(MOBA_TOPK)], axis=1)
        s = jnp.dot(q.astype(BF16), kt, preferred_element_type=F32) - float(slopes[hh]) * dist
        s0 = jnp.sum(q * kn, axis=-1, keepdims=True)
        m = jnp.maximum(jnp.max(s, axis=-1, keepdims=True), s0)
        e = jnp.exp(s - m)
        e0 = jnp.exp(s0 - m)
        den = jnp.sum(e, axis=-1, keepdims=True) + e0
        outs.append((_nt_dot(e.astype(BF16), vt) + e0 * vn) / den)
    o_ref[0] = jnp.concatenate(outs, axis=0)


def _dec_moba_att(pt_flat, sel_flat, q, nk, nv, cache_k, cache_v, layer, n_pages):
    bd = q.shape[0]
    ppb = MOBA_BLOCK // PAGE
    n_sel = MOBA_HEADS * MOBA_TOPK
    b3 = lambda b, c, pt, sel: (b, 0, 0)
    in_specs = [pl.BlockSpec((1, MOBA_HEADS, HEAD_DIM), b3), pl.BlockSpec((1, 1, HEAD_DIM), b3),
                pl.BlockSpec((1, 1, HEAD_DIM), b3)] + [pl.BlockSpec(memory_space=pl.ANY)] * 2
    grid_spec = pltpu.PrefetchScalarGridSpec(
        num_scalar_prefetch=2, grid=(bd, 1), in_specs=in_specs,
        out_specs=pl.BlockSpec((1, MOBA_HEADS, HEAD_DIM), b3),
        scratch_shapes=_paged_scratch([(HEAD_DIM, PAGE), (HEAD_DIM, PAGE)], n_sel * ppb))
    return pl.pallas_call(
        functools.partial(_dec_moba_att_body, past=n_pages * PAGE, layer=layer, n_pages=n_pages),
        grid_spec=grid_spec, out_shape=jax.ShapeDtypeStruct((bd, MOBA_HEADS, HEAD_DIM), F32),
        compiler_params=_cparams(2), name="dec_moba_att",
    )(pt_flat, sel_flat, q.astype(F32), nk, nv, cache_k, cache_v)


def _swap_halves(w):
    half = w.shape[-1] // 2
    return jnp.concatenate([w[..., half:], w[..., :half]], axis=-1)


def _layer_weights(l, w_in, b_forget, mla_q_norm, w_q_up, mla_kv_norm, w_uk, w_uv, w_out, ln_mix_g, ln_mix_b,
                   w_gate, w_up, w_down, ln_ffn_g, ln_ffn_b):
    w = w_in[l]
    splits = (MLA_Q_LORA, MLA_KV_LORA, MLA_ROPE, FOX_HEADS * HEAD_DIM, HEAD_DIM, HEAD_DIM, FOX_HEADS,
              MOBA_HEADS * HEAD_DIM, HEAD_DIM, HEAD_DIM)
    offs = np.cumsum((0,) + splits)
    cq, ckv, kpe, fq, fk, fv, fg, mq, mk, mv = [w[:, offs[i]:offs[i + 1]] for i in range(len(splits))]
    zeros = lambda n: jnp.zeros((D_MODEL, n), w.dtype)

    def head_chunks(wq, heads):
        return jnp.concatenate([jnp.concatenate([wq[:, HEAD_DIM * hh:HEAD_DIM * (hh + 1)], zeros(LANES - HEAD_DIM)], 1)
                                for hh in range(heads)], axis=1)

    wall = jnp.concatenate([cq, ckv, jnp.tile(kpe, (1, 4)), jnp.tile(_swap_halves(kpe), (1, 4)),
                            head_chunks(fq, FOX_HEADS), head_chunks(mq, MOBA_HEADS),
                            fk, fv, mk, mv, fg, zeros(LANES - FOX_HEADS)], axis=1)
    assert wall.shape[1] == N_COLS
    wq = w_q_up[l]
    q_nope = wq[:, :, :MLA_NOPE].reshape(MLA_Q_LORA, MLA_HEADS * MLA_NOPE)
    q_pe = wq[:, :, MLA_NOPE:]
    wq2 = jnp.concatenate([q_nope, q_pe.reshape(MLA_Q_LORA, -1), _swap_halves(q_pe).reshape(MLA_Q_LORA, -1)], axis=1)
    wuk_t = jnp.transpose(w_uk[l], (1, 2, 0))
    zpad = jnp.zeros_like(wuk_t)
    even = jnp.concatenate([wuk_t, zpad], axis=1)
    odd = jnp.concatenate([zpad, wuk_t], axis=1)
    is_odd = (jnp.arange(MLA_HEADS) % 2 == 1)[:, None, None]
    wuk = jnp.where(is_odd, odd, even)
    bfg = jnp.zeros((1, LANES), F32).at[0, :FOX_HEADS].set(b_forget[l])
    return dict(
        wall=wall.astype(BF16), gq=mla_q_norm[l][None, :], gkv=mla_kv_norm[l][None, :], wq2=wq2.astype(BF16),
        wuk=wuk.astype(BF16), bfg=bfg, wuv=jnp.transpose(w_uv[l], (1, 0, 2)).astype(BF16),
        wout=w_out[l].astype(BF16), ln_mix_g=ln_mix_g[l][None, :], ln_mix_b=ln_mix_b[l][None, :],
        wgate=w_gate[l].astype(BF16), wup=w_up[l].astype(BF16), wdown=w_down[l].astype(BF16),
        ln_ffn_g=ln_ffn_g[l][None, :], ln_ffn_b=ln_ffn_b[l][None, :])


def _rope_tables(pos):
    half = MLA_ROPE // 2
    inv = ROPE_THETA ** (-jnp.arange(half, dtype=F32) / half)
    ang = pos.astype(F32)[:, None] * inv[None, :]
    cos = jnp.concatenate([jnp.cos(ang), jnp.cos(ang)], axis=-1)
    sin = jnp.concatenate([-jnp.sin(ang), jnp.sin(ang)], axis=-1)
    return (jnp.tile(cos, (1, 4)), jnp.tile(sin, (1, 4)), jnp.tile(cos, (1, MLA_HEADS)), jnp.tile(sin, (1, MLA_HEADS)))


def _pick(n, prefs):
    for p in prefs:
        if n % p == 0:
            return p
    raise ValueError("no tile for %d" % n)


def kernel(x_prompt, x_sample, cache_mla_ckv, cache_mla_kpe, cache_fox_k, cache_fox_v, cache_fox_logf, cache_moba_k,
           cache_moba_v, page_table, w_in, b_forget, mla_q_norm, w_q_up, mla_kv_norm, w_uk, w_uv, w_out, ln_mix_g,
           ln_mix_b, w_gate, w_up, w_down, ln_ffn_g, ln_ffn_b):
    nb, sp, _ = x_prompt.shape
    bd, dec_t, _ = x_sample.shape
    assert dec_t == 1 and sp % MOBA_BLOCK == 0
    n_pages = page_table.shape[1]
    past = n_pages * PAGE
    pt_flat = page_table.reshape(-1).astype(jnp.int32)
    nblk = sp // MOBA_BLOCK

    tabs_p = _rope_tables(jnp.arange(sp, dtype=jnp.int32))
    tabs_s = _rope_tables(jnp.full((bd,), past, jnp.int32))
    tm_p = _pick(sp, (512, 256))
    tq = 256
    tk = _pick(sp, (2048, 1024, 512, 256))
    pg = _pick(n_pages, (32, 16, 8, 4, 2))
    kpe_t, fk_t, fv_t, lf_t, mk_t, mv_t = [jnp.swapaxes(a, 2, 3) for a in (
        cache_mla_kpe, cache_fox_k, cache_fox_v, cache_fox_logf, cache_moba_k, cache_moba_v)]

    xp = x_prompt.reshape(nb * sp, D_MODEL)
    xs = x_sample.reshape(bd, D_MODEL)
    new_p = [[] for _ in range(7)]
    new_s = [[] for _ in range(7)]

    for l in range(DEPTH):
        lw = _layer_weights(l, w_in, b_forget, mla_q_norm, w_q_up, mla_kv_norm, w_uk, w_uv, w_out, ln_mix_g,
                            ln_mix_b, w_gate, w_up, w_down, ln_ffn_g, ln_ffn_b)

        qmla, kmla, ckv, kpe, qf, qm, fkv, mkv, logf = _proj(xp, tabs_p, lw, nb, sp, tm_p)
        qfox, kfox = _fox_prep(logf, qf, fkv, nb, sp, tm_p)
        means = _moba_means(mkv, nb, nblk)[:, :, 0, :HEAD_DIM]
        kmean_pad = jnp.zeros((nb, LANES, LANES), F32).at[:, MOBA_BLK:MOBA_BLK + nblk, :HEAD_DIM].set(means)
        qmoba, kmoba = _moba_prep(qm, kmean_pad, mkv, nb, sp)
        fk, fv = fkv[:, :HEAD_DIM], fkv[:, HEAD_DIM:]
        mk, mv = mkv[:, :HEAD_DIM], mkv[:, HEAD_DIM:]
        o_mla = _flash(qmla, kmla, None, tq=tq, tk=tk, dv=MLA_KV_LORA, groups=MLA_GROUPS, sub=tk)
        o_fox = _flash(qfox, kfox, fv.astype(BF16), tq=tq, tk=tk, dv=HEAD_DIM, groups=NARROW_GROUPS, sub=tk)
        o_moba = _flash(qmoba, kmoba, mv.astype(BF16), tq=tq, tk=tk, dv=HEAD_DIM, groups=NARROW_GROUPS, sub=tk)
        x1 = _mixout(xp, o_mla, o_fox, o_moba, lw, nb, sp, tq)
        xp = _ffn(x1, lw, tm_p, D_FF // 2)
        for lst, a, shp in zip(new_p, (ckv, kpe[:, :MLA_ROPE], fk, fv, logf[:, :FOX_HEADS], mk, mv),
                               (MLA_KV_LORA, MLA_ROPE, HEAD_DIM, HEAD_DIM, FOX_HEADS, HEAD_DIM, HEAD_DIM)):
            lst.append(a.reshape(nb, sp, shp))

        qmla, _, sckv, skpe, qf, qm, sfkv, smkv, slogf = _proj(xs, tabs_s, lw, 1, bd, bd)
        skpe = skpe[:, :MLA_ROPE]
        sfk, sfv = sfkv[:, :HEAD_DIM], sfkv[:, HEAD_DIM:]
        smk, smv = smkv[:, :HEAD_DIM], smkv[:, HEAD_DIM:]
        slogf = slogf[:, :FOX_HEADS]
        qs = jnp.transpose(qmla[0], (1, 0, 2))
        qlat_s = qs[:, :, :MLA_KV_LORA]
        qpe_s = qs[:, :, MLA_KV_LORA:].reshape(bd, MLA_HEADS, 4, MLA_ROPE).sum(axis=2)
        fq_s = jnp.transpose(qf[0, :, :, :HEAD_DIM], (1, 0, 2))
        mq_s = jnp.transpose(qm[0, :, :, :HEAD_DIM], (1, 0, 2))
        o_mla_s = _dec_mla(pt_flat, qlat_s, qpe_s, sckv[:, None, :], skpe[:, None, :], cache_mla_ckv, kpe_t,
                           l, n_pages, pg)
        o_fox_s = _dec_fox(pt_flat, fq_s, sfk[:, None, :], sfv[:, None, :], slogf[:, :, None], fk_t, fv_t, lf_t,
                           l, n_pages, pg)
        sel = _dec_moba_sel(pt_flat, mq_s, mk_t, l, n_pages, pg)[:, :, :MOBA_TOPK]
        o_moba_s = _dec_moba_att(pt_flat, sel.reshape(-1), mq_s, smk[:, None, :], smv[:, None, :], mk_t, mv_t,
                                 l, n_pages)
        to_heads = lambda o: jnp.transpose(o, (1, 0, 2))[None]
        x1s = _mixout(xs, to_heads(o_mla_s), to_heads(o_fox_s), to_heads(o_moba_s), lw, 1, bd, bd)
        xs = _ffn(x1s, lw, bd, D_FF // 2)
        for lst, a in zip(new_s, (sckv, skpe, sfk, sfv, slogf, smk, smv)):
            lst.append(a.reshape(bd, 1, a.shape[-1]))

    outs = [xp.reshape(nb, sp, D_MODEL), xs.reshape(bd, 1, D_MODEL)]
    for a, b in zip(new_p, new_s):
        outs.append(jnp.stack(a, 0))
        outs.append(jnp.stack(b, 0))
    return tuple(outs)
```
